```python
import math
import jax, jax.numpy as jnp
from jax import lax
import numpy as np

D_MODEL = 1024
BATCH = 8
SEQ = 4096
DEPTH = 2

N_META = 16
S5_GROUP = 16
S5_GROUPS = D_MODEL // S5_GROUP
S5_STATE = 64
N_HEADS = 16
HEAD_DIM = D_MODEL // N_HEADS
D_FF = ((8 * D_MODEL + 3 * 256 - 1) // (3 * 256)) * 256
Q_BLOCK = 128
N_A_LAYERS = DEPTH // 2
N_B_LAYERS = DEPTH - N_A_LAYERS
RMS_EPS = 1e-6
DT_MIN = 1e-3
DT_MAX = 1e-1

kernel_name = "s5_yoco_stickbreaking_hybrid"


def rmsnorm(x, gain):
    x32 = x.astype(jnp.float32)
    y = x32 * lax.rsqrt(jnp.mean(x32 * x32, axis=-1, keepdims=True) + RMS_EPS)
    return (y * gain.astype(jnp.float32)).astype(x.dtype)


def s5_mixer(u, a_re, a_im, log_dt, b_re, b_im, c_re, c_im, d_skip, w_glu):
    bsz, L, _ = u.shape
    f32 = jnp.float32
    u32 = u.astype(f32).reshape(bsz, L, S5_GROUPS, S5_GROUP)
    a_re = a_re.astype(f32)
    a_im = a_im.astype(f32)
    dt = jnp.exp(log_dt.astype(f32))[:, None]
    mag = jnp.exp(dt * a_re)
    ang = dt * a_im
    abar_re = mag * jnp.cos(ang)
    abar_im = mag * jnp.sin(ang)
    den = a_re * a_re + a_im * a_im
    coef_re = ((abar_re - 1.0) * a_re + abar_im * a_im) / den
    coef_im = (abar_im * a_re - (abar_re - 1.0) * a_im) / den
    b_re = b_re.astype(f32)
    b_im = b_im.astype(f32)
    bbar_re = coef_re[..., None] * b_re - coef_im[..., None] * b_im
    bbar_im = coef_re[..., None] * b_im + coef_im[..., None] * b_re
    bu_re = jnp.einsum('blgc,gpc->blgp', u32, bbar_re)
    bu_im = jnp.einsum('blgc,gpc->blgp', u32, bbar_im)
    a_seq_re = jnp.broadcast_to(abar_re, (1, L, S5_GROUPS, S5_STATE))
    a_seq_im = jnp.broadcast_to(abar_im, (1, L, S5_GROUPS, S5_STATE))

    def combine(e1, e2):
        a1r, a1i, b1r, b1i = e1
        a2r, a2i, b2r, b2i = e2
        return (a2r * a1r - a2i * a1i,
                a2r * a1i + a2i * a1r,
                a2r * b1r - a2i * b1i + b2r,
                a2r * b1i + a2i * b1r + b2i)

    _, _, x_re, x_im = lax.associative_scan(combine, (a_seq_re, a_seq_im, bu_re, bu_im), axis=1)
    y = (jnp.einsum('blgp,gcp->blgc', x_re, c_re.astype(f32))
         - jnp.einsum('blgp,gcp->blgc', x_im, c_im.astype(f32)))
    y = (y + d_skip.astype(f32).reshape(S5_GROUPS, S5_GROUP) * u32).reshape(bsz, L, D_MODEL)
    z = jax.nn.gelu(y)
    vg = jnp.einsum('bld,de->ble', z, w_glu.astype(f32))
    val, gate = jnp.split(vg, 2, axis=-1)
    return (val * jax.nn.sigmoid(gate)).astype(u.dtype)


def stick_breaking_attention(q, k, v):
    f32 = jnp.float32
    q = q.astype(f32)
    k = k.astype(f32)
    v = v.astype(f32)
    L = q.shape[1]
    scale = 1.0 / math.sqrt(HEAD_DIM)
    n_real_blocks = (L - N_META) // Q_BLOCK
    bounds = [(0, N_META)] + [(N_META + i * Q_BLOCK, N_META + (i + 1) * Q_BLOCK) for i in range(n_real_blocks)]
    outs = []
    for q0, q1 in bounds:
        qb = q[:, q0:q1]
        kb = k[:, :q1]
        vb = v[:, :q1]
        z = jnp.einsum('bqhd,bkhd->bhqk', qb, kb) * scale
        t_idx = jnp.arange(q0, q1)[:, None]
        s_idx = jnp.arange(q1)[None, :]
        strict = s_idx < t_idx
        log_beta = jax.nn.log_sigmoid(z)
        log_1m = jnp.where(strict, jax.nn.log_sigmoid(-z), 0.0)
        log_remain = lax.cumsum(log_1m, axis=3, reverse=True) - log_1m
        w = jnp.where(strict, jnp.exp(log_beta + log_remain), 0.0)
        outs.append(jnp.einsum('bhqk,bkhd->bqhd', w, vb))
    return jnp.concatenate(outs, axis=1)


def swiglu_ffn(h, w_in, w_out):
    gu = jnp.einsum('bld,df->blf', h, w_in)
    g, u = jnp.split(gu, 2, axis=-1)
    return jnp.einsum('blf,fd->bld', jax.nn.silu(g) * u, w_out)


def setup_inputs(seed: int = 0) -> dict:
    key = jax.random.key(seed)
    ks = jax.random.split(key, 24)
    f32 = jnp.float32
    G, P, C = S5_GROUPS, S5_STATE, S5_GROUP
    HD = N_HEADS * HEAD_DIM
    x = jax.random.normal(ks[0], (BATCH, SEQ, D_MODEL), f32)
    meta_tokens = jax.random.normal(ks[1], (N_META, D_MODEL), f32)
    norm_mix = 1.0 + 0.02 * jax.random.normal(ks[2], (DEPTH, D_MODEL), f32)
    norm_ffn = 1.0 + 0.02 * jax.random.normal(ks[3], (DEPTH, D_MODEL), f32)
    n_idx = jnp.arange(P, dtype=f32)
    s5_a_re = -0.5 + 0.01 * jax.random.normal(ks[4], (N_A_LAYERS, G, P), f32)
    s5_a_im = math.pi * n_idx + 0.01 * jax.random.normal(ks[5], (N_A_LAYERS, G, P), f32)
    s5_log_dt = jax.random.uniform(ks[6], (N_A_LAYERS, G), f32, math.log(DT_MIN), math.log(DT_MAX))
    s5_b_re = jax.random.normal(ks[7], (N_A_LAYERS, G, P, C), f32) * (2 * C) ** -0.5
    s5_b_im = jax.random.normal(ks[8], (N_A_LAYERS, G, P, C), f32) * (2 * C) ** -0.5
    s5_c_re = jax.random.normal(ks[9], (N_A_LAYERS, G, C, P), f32) * P ** -0.5
    s5_c_im = jax.random.normal(ks[10], (N_A_LAYERS, G, C, P), f32) * P ** -0.5
    s5_d = jax.random.normal(ks[11], (N_A_LAYERS, D_MODEL), f32)
    s5_w_glu = jax.random.normal(ks[12], (N_A_LAYERS, D_MODEL, 2 * D_MODEL), f32) * D_MODEL ** -0.5
    norm_kv = 1.0 + 0.02 * jax.random.normal(ks[13], (D_MODEL,), f32)
    w_kv = jax.random.normal(ks[14], (D_MODEL, 2 * HD), f32) * D_MODEL ** -0.5
    w_q = jax.random.normal(ks[15], (N_B_LAYERS, D_MODEL, HD), f32) * D_MODEL ** -0.5
    w_o = jax.random.normal(ks[16], (N_B_LAYERS, HD, D_MODEL), f32) * HD ** -0.5
    w_ffn_in = jax.random.normal(ks[17], (DEPTH, D_MODEL, 2 * D_FF), f32) * D_MODEL ** -0.5
    w_ffn_out = jax.random.normal(ks[18], (DEPTH, D_FF, D_MODEL), f32) * D_FF ** -0.5
    norm_final = 1.0 + 0.02 * jax.random.normal(ks[19], (D_MODEL,), f32)
    return {"x": x, "meta_tokens": meta_tokens, "norm_mix": norm_mix, "norm_ffn": norm_ffn,
            "s5_a_re": s5_a_re, "s5_a_im": s5_a_im, "s5_log_dt": s5_log_dt,
            "s5_b_re": s5_b_re, "s5_b_im": s5_b_im, "s5_c_re": s5_c_re, "s5_c_im": s5_c_im,
            "s5_d": s5_d, "s5_w_glu": s5_w_glu, "norm_kv": norm_kv, "w_kv": w_kv,
            "w_q": w_q, "w_o": w_o, "w_ffn_in": w_ffn_in, "w_ffn_out": w_ffn_out,
            "norm_final": norm_final}


def reference(x, meta_tokens, norm_mix, norm_ffn, s5_a_re, s5_a_im, s5_log_dt,
              s5_b_re, s5_b_im, s5_c_re, s5_c_im, s5_d, s5_w_glu, norm_kv, w_kv,
              w_q, w_o, w_ffn_in, w_ffn_out, norm_final):
    bsz = x.shape[0]
    meta = jnp.broadcast_to(meta_tokens.astype(x.dtype)[None], (bsz, N_META, D_MODEL))
    h = jnp.concatenate([meta, x], axis=1)
    L = h.shape[1]
    k_shared = None
    v_shared = None
    for i in range(DEPTH):
        if i < N_A_LAYERS:
            a = i
            h = h + s5_mixer(rmsnorm(h, norm_mix[i]), s5_a_re[a], s5_a_im[a], s5_log_dt[a],
                             s5_b_re[a], s5_b_im[a], s5_c_re[a], s5_c_im[a], s5_d[a], s5_w_glu[a])
        else:
            j = i - N_A_LAYERS
            q = jnp.einsum('bld,de->ble', rmsnorm(h, norm_mix[i]), w_q[j]).reshape(bsz, L, N_HEADS, HEAD_DIM)
            o = stick_breaking_attention(q, k_shared, v_shared).astype(h.dtype)
            h = h + jnp.einsum('ble,ed->bld', o.reshape(bsz, L, N_HEADS * HEAD_DIM), w_o[j])
        h = h + swiglu_ffn(rmsnorm(h, norm_ffn[i]), w_ffn_in[i], w_ffn_out[i])
        if i == N_A_LAYERS - 1:
            kv = jnp.einsum('bld,de->ble', rmsnorm(h, norm_kv), w_kv).reshape(bsz, L, 2, N_HEADS, HEAD_DIM)
            k_shared = kv[:, :, 0]
            v_shared = kv[:, :, 1]
    out = rmsnorm(h, norm_final)
    return out[:, N_META:]
```

```python
import functools
import math

import jax
import jax.numpy as jnp
from jax import lax
from jax.experimental import pallas as pl
from jax.experimental.pallas import tpu as pltpu

D_MODEL = 1024
BATCH = 8
SEQ = 4096
N_META = 16
S5_GROUP = 16
S5_GROUPS = 64
S5_STATE = 64
N_HEADS = 16
HEAD_DIM = 64
D_FF = 2816
RMS_EPS = 1e-6

LANES = 128
SUBLANES = 8
MXU_DIM = 256
VMEM_LIMIT_BYTES = 56 * 1024 * 1024

S5_SLAB_GROUPS = MXU_DIM // S5_GROUP
S5_SLABS = S5_GROUPS // S5_SLAB_GROUPS
S5_SLAB_STATE = S5_SLAB_GROUPS * S5_STATE
S5_CHUNK = 64

FFN_ROWS = 512
FFN_F_CHUNKS = ((0, 1024), (1024, 2048), (2048, D_FF))
PROJ_ROWS = 512

ATT_Q = 256
ATT_K = 256
META_PAD = 128

F32 = jnp.float32
BF16 = jnp.bfloat16


def _rmsnorm(x, gain):
    ms = jnp.mean(x * x, axis=-1, keepdims=True)
    return x * lax.rsqrt(ms + RMS_EPS) * gain


def _gelu_tanh(y):
    c = math.sqrt(2.0 / math.pi)
    return 0.5 * y * (1.0 + jnp.tanh(c * (y + 0.044715 * (y * y * y))))


def _sigmoid(x):
    return 1.0 / (1.0 + jnp.exp(-x))


def _const_spec(shape):
    nd = len(shape)
    return pl.BlockSpec(shape, lambda *_: (0,) * nd, pipeline_mode=pl.Buffered(1))


def _s5_kernel(h_ref, gain_ref, bw_ref, avec_ref, cw_ref, d_ref, wglu_ref, init_ref,
               out_ref, fin_ref, state_ref, bu_ref, y_ref, *, chunk):
    half = S5_SLAB_STATE

    @pl.when(pl.program_id(0) == 0)
    def _():
        state_ref[...] = init_ref[...]

    h = h_ref[...]
    u = _rmsnorm(h, gain_ref[...])
    ub = u.astype(BF16)
    for s in range(S5_SLABS):
        bu_ref[...] = jnp.dot(ub[:, s * MXU_DIM:(s + 1) * MXU_DIM], bw_ref[s],
                              preferred_element_type=F32)
        a_re = avec_ref[s, :, :half]
        a_im = avec_ref[s, :, half:]

        def step(l, carry, a_re=a_re, a_im=a_im):
            x_re, x_im = carry
            row = pl.multiple_of(l * SUBLANES, SUBLANES)
            b_re = bu_ref[pl.ds(row, SUBLANES), :half]
            b_im = bu_ref[pl.ds(row, SUBLANES), half:]
            n_re = a_re * x_re - a_im * x_im + b_re
            n_im = a_re * x_im + a_im * x_re + b_im
            bu_ref[pl.ds(row, SUBLANES), :half] = n_re
            bu_ref[pl.ds(row, SUBLANES), half:] = n_im
            return n_re, n_im

        x_re, x_im = lax.fori_loop(0, chunk, step,
                                   (state_ref[s, :, :half], state_ref[s, :, half:]), unroll=2)
        state_ref[s, :, :half] = x_re
        state_ref[s, :, half:] = x_im
        y_ref[:, s * MXU_DIM:(s + 1) * MXU_DIM] = jnp.dot(
            bu_ref[...].astype(BF16), cw_ref[s], preferred_element_type=F32)

    y = y_ref[...] + d_ref[...] * u
    z = _gelu_tanh(y).astype(BF16)
    vg = jnp.dot(z, wglu_ref[...], preferred_element_type=F32)
    out_ref[...] = h + vg[:, :D_MODEL] * _sigmoid(vg[:, D_MODEL:])

    @pl.when(pl.program_id(0) == pl.num_programs(0) - 1)
    def _():
        fin_ref[...] = state_ref[...]


def _s5_layer(h, gain, bw, avec, cw, d_skip, wglu, init_state, *, chunk):
    rows = h.shape[0]
    m = chunk * BATCH
    assert rows % m == 0
    state_shape = (S5_SLABS, BATCH, 2 * S5_SLAB_STATE)
    return pl.pallas_call(
        functools.partial(_s5_kernel, chunk=chunk),
        grid=(rows // m,),
        in_specs=[
            pl.BlockSpec((m, D_MODEL), lambda i: (i, 0)),
            _const_spec((1, D_MODEL)),
            _const_spec(bw.shape),
            _const_spec(avec.shape),
            _const_spec(cw.shape),
            _const_spec((1, D_MODEL)),
            _const_spec(wglu.shape),
            _const_spec(state_shape),
        ],
        out_specs=[
            pl.BlockSpec((m, D_MODEL), lambda i: (i, 0)),
            pl.BlockSpec(state_shape, lambda i: (0, 0, 0)),
        ],
        out_shape=[
            jax.ShapeDtypeStruct((rows, D_MODEL), F32),
            jax.ShapeDtypeStruct(state_shape, F32),
        ],
        scratch_shapes=[
            pltpu.VMEM(state_shape, F32),
            pltpu.VMEM((m, 2 * S5_SLAB_STATE), F32),
            pltpu.VMEM((m, D_MODEL), F32),
        ],
        compiler_params=pltpu.CompilerParams(
            dimension_semantics=("arbitrary",), vmem_limit_bytes=VMEM_LIMIT_BYTES),
        name="s5_layer",
    )(h, gain, bw, avec, cw, d_skip, wglu, init_state)


def _swiglu(hn, wg_ref, wu_ref, wo_ref):
    acc = None
    for f0, f1 in FFN_F_CHUNKS:
        g = jnp.dot(hn, wg_ref[:, f0:f1], preferred_element_type=F32)
        u = jnp.dot(hn, wu_ref[:, f0:f1], preferred_element_type=F32)
        a = (g * _sigmoid(g) * u).astype(BF16)
        part = jnp.dot(a, wo_ref[f0:f1, :], preferred_element_type=F32)
        acc = part if acc is None else acc + part
    return acc


def _ffn_kernel(h_ref, gain_ref, wg_ref, wu_ref, wo_ref, out_ref):
    h = h_ref[...]
    hn = _rmsnorm(h, gain_ref[...]).astype(BF16)
    out_ref[...] = h + _swiglu(hn, wg_ref, wu_ref, wo_ref)


def _ffn_layer(h, gain, wg, wu, wo, *, rows_per_step):
    rows = h.shape[0]
    assert rows % rows_per_step == 0
    return pl.pallas_call(
        _ffn_kernel,
        grid=(rows // rows_per_step,),
        in_specs=[
            pl.BlockSpec((rows_per_step, D_MODEL), lambda i: (i, 0)),
            _const_spec((1, D_MODEL)),
            _const_spec(wg.shape),
            _const_spec(wu.shape),
            _const_spec(wo.shape),
        ],
        out_specs=pl.BlockSpec((rows_per_step, D_MODEL), lambda i: (i, 0)),
        out_shape=jax.ShapeDtypeStruct((rows, D_MODEL), F32),
        compiler_params=pltpu.CompilerParams(
            dimension_semantics=("parallel",), vmem_limit_bytes=VMEM_LIMIT_BYTES),
        name="ffn_layer",
    )(h, gain, wg, wu, wo)


def _proj_kernel(h_ref, gq_ref, gkv_ref, wq_ref, wk_ref, wv_ref, q_ref, k_ref, v_ref):
    h = h_ref[...]
    y = h * lax.rsqrt(jnp.mean(h * h, axis=-1, keepdims=True) + RMS_EPS)
    hq = (y * gq_ref[...]).astype(BF16)
    hkv = (y * gkv_ref[...]).astype(BF16)
    scale = 1.0 / math.sqrt(HEAD_DIM)
    q_ref[...] = (jnp.dot(hq, wq_ref[...], preferred_element_type=F32) * scale).astype(BF16)
    k_ref[...] = jnp.dot(hkv, wk_ref[...], preferred_element_type=F32).astype(BF16)
    v_ref[...] = jnp.dot(hkv, wv_ref[...], preferred_element_type=F32).astype(BF16)


def _proj_layer(h, gq, gkv, wq, wk, wv, *, rows_per_step):
    rows = h.shape[0]
    assert rows % rows_per_step == 0
    hd = N_HEADS * HEAD_DIM
    row_spec = pl.BlockSpec((rows_per_step, hd), lambda i: (i, 0))
    return pl.pallas_call(
        _proj_kernel,
        grid=(rows // rows_per_step,),
        in_specs=[
            pl.BlockSpec((rows_per_step, D_MODEL), lambda i: (i, 0)),
            _const_spec((1, D_MODEL)),
            _const_spec((1, D_MODEL)),
            _const_spec(wq.shape),
            _const_spec(wk.shape),
            _const_spec(wv.shape),
        ],
        out_specs=[row_spec, row_spec, row_spec],
        out_shape=[jax.ShapeDtypeStruct((rows, hd), BF16)] * 3,
        compiler_params=pltpu.CompilerParams(
            dimension_semantics=("parallel",), vmem_limit_bytes=VMEM_LIMIT_BYTES),
        name="qkv_proj",
    )(h, gq, gkv, wq, wk, wv)


def _attn_kernel(q_ref, k_ref, v_ref, km_ref, vm_ref, tri_ref, trim_ref, o_ref, acc_ref, rem_ref):
    qi = pl.program_id(2)
    q = q_ref[...]
    lane = lax.broadcasted_iota(jnp.int32, (1, LANES), 1)
    head0 = lane < HEAD_DIM
    zero = jnp.zeros_like(q)
    q_heads = (jnp.where(head0, q, zero), jnp.where(head0, zero, q))
    acc_ref[...] = jnp.zeros_like(acc_ref)
    rem_ref[...] = jnp.zeros_like(rem_ref)

    def block(kb, vb, tri, mask):
        nk = kb.shape[0]
        for hd in range(2):
            z = lax.dot_general(q_heads[hd], kb, (((1,), (1,)), ((), ())),
                                preferred_element_type=F32)
            sp = jnp.maximum(z, 0.0) + jnp.log(1.0 + jnp.exp(-jnp.abs(z)))
            l1m = -sp
            if mask is not None:
                l1m = jnp.where(mask, l1m, 0.0)
            hi = l1m.astype(BF16)
            lo = (l1m - hi.astype(F32)).astype(BF16)
            suffix = jnp.dot(jnp.concatenate([hi, lo], axis=1), tri, preferred_element_type=F32)
            rem = rem_ref[hd]
            rem_b = jnp.concatenate([rem] * (nk // LANES), axis=1)
            w = jnp.exp((z - sp) + suffix + rem_b)
            if mask is not None:
                w = jnp.where(mask, w, 0.0)
            acc_ref[hd] += jnp.dot(w.astype(BF16), vb, preferred_element_type=F32)
            rem_ref[hd] = rem + jnp.sum(l1m, axis=1, keepdims=True)

    row = lax.broadcasted_iota(jnp.int32, (ATT_Q, ATT_K), 0)
    col = lax.broadcasted_iota(jnp.int32, (ATT_Q, ATT_K), 1)
    d0 = pl.multiple_of(qi * ATT_Q, ATT_Q)
    block(k_ref[pl.ds(d0, ATT_K), :], v_ref[pl.ds(d0, ATT_K), :], tri_ref[...], col < row)

    def earlier(i, carry):
        k0 = pl.multiple_of((qi - 1 - i) * ATT_K, ATT_K)
        block(k_ref[pl.ds(k0, ATT_K), :], v_ref[pl.ds(k0, ATT_K), :], tri_ref[...], None)
        return carry

    lax.fori_loop(0, qi, earlier, 0)

    mcol = lax.broadcasted_iota(jnp.int32, (ATT_Q, META_PAD), 1)
    block(km_ref[...], vm_ref[...], trim_ref[...], mcol < N_META)

    o_ref[...] = jnp.where(head0, acc_ref[0], acc_ref[1]).astype(o_ref.dtype)


def _attention(q, k, v, k_meta, v_meta, tri, tri_meta):
    pairs = N_HEADS * HEAD_DIM // LANES
    col = lambda b, p, i: b * pairs + p
    return pl.pallas_call(
        _attn_kernel,
        grid=(BATCH, pairs, SEQ // ATT_Q),
        in_specs=[
            pl.BlockSpec((ATT_Q, LANES), lambda b, p, i: (i, col(b, p, i))),
            pl.BlockSpec((SEQ, LANES), lambda b, p, i: (0, col(b, p, i))),
            pl.BlockSpec((SEQ, LANES), lambda b, p, i: (0, col(b, p, i))),
            pl.BlockSpec((META_PAD, LANES), lambda b, p, i: (0, p)),
            pl.BlockSpec((META_PAD, LANES), lambda b, p, i: (0, p)),
            _const_spec(tri.shape),
            _const_spec(tri_meta.shape),
        ],
        out_specs=pl.BlockSpec((ATT_Q, LANES), lambda b, p, i: (i, col(b, p, i))),
        out_shape=jax.ShapeDtypeStruct(q.shape, BF16),
        scratch_shapes=[
            pltpu.VMEM((2, ATT_Q, LANES), F32),
            pltpu.VMEM((2, ATT_Q, LANES), F32),
        ],
        compiler_params=pltpu.CompilerParams(
            dimension_semantics=("parallel", "parallel", "arbitrary"),
            vmem_limit_bytes=VMEM_LIMIT_BYTES),
        name="stickbreak_attn",
    )(q, k, v, k_meta, v_meta, tri, tri_meta)


def _out_kernel(h_ref, o_ref, wo_att_ref, gain_ref, wg_ref, wu_ref, wo_ref, gfin_ref, out_ref):
    h = h_ref[...] + jnp.dot(o_ref[...], wo_att_ref[...], preferred_element_type=F32)
    hn = _rmsnorm(h, gain_ref[...]).astype(BF16)
    h = h + _swiglu(hn, wg_ref, wu_ref, wo_ref)
    out_ref[...] = _rmsnorm(h, gfin_ref[...])


def _out_layer(h, o, wo_att, gain, wg, wu, wo, gfin, *, rows_per_step):
    return pl.pallas_call(
        _out_kernel,
        grid=(BATCH, SEQ // rows_per_step),
        in_specs=[
            pl.BlockSpec((rows_per_step, D_MODEL), lambda b, i: (i, b)),
            pl.BlockSpec((rows_per_step, D_MODEL), lambda b, i: (i, b)),
            _const_spec(wo_att.shape),
            _const_spec((1, D_MODEL)),
            _const_spec(wg.shape),
            _const_spec(wu.shape),
            _const_spec(wo.shape),
            _const_spec((1, D_MODEL)),
        ],
        out_specs=pl.BlockSpec((None, rows_per_step, D_MODEL), lambda b, i: (b, i, 0)),
        out_shape=jax.ShapeDtypeStruct((BATCH, SEQ, D_MODEL), F32),
        compiler_params=pltpu.CompilerParams(
            dimension_semantics=("parallel", "parallel"), vmem_limit_bytes=VMEM_LIMIT_BYTES),
        name="oproj_ffn_final",
    )(h, o, wo_att, gain, wg, wu, wo, gfin)


def _s5_weights(a_re, a_im, log_dt, b_re, b_im, c_re, c_im):
    dt = jnp.exp(log_dt)[:, None]
    mag = jnp.exp(dt * a_re)
    ang = dt * a_im
    abar_re = mag * jnp.cos(ang)
    abar_im = mag * jnp.sin(ang)
    den = a_re * a_re + a_im * a_im
    coef_re = ((abar_re - 1.0) * a_re + abar_im * a_im) / den
    coef_im = (abar_im * a_re - (abar_re - 1.0) * a_im) / den
    bbar_re = coef_re[..., None] * b_re - coef_im[..., None] * b_im
    bbar_im = coef_re[..., None] * b_im + coef_im[..., None] * b_re
    eye = jnp.eye(S5_SLAB_GROUPS, dtype=F32)
    slab = (S5_SLABS, S5_SLAB_GROUPS)

    def b_slab(bb):
        w = jnp.einsum('sgpc,gh->sgchp', bb.reshape(slab + (S5_STATE, S5_GROUP)), eye)
        return w.reshape(S5_SLABS, MXU_DIM, S5_SLAB_STATE)

    def c_slab(cc):
        w = jnp.einsum('sgcp,gh->sgphc', cc.reshape(slab + (S5_GROUP, S5_STATE)), eye)
        return w.reshape(S5_SLABS, S5_SLAB_STATE, MXU_DIM)

    bw = jnp.concatenate([b_slab(bbar_re), b_slab(bbar_im)], axis=2).astype(BF16)
    cw = jnp.concatenate([c_slab(c_re), -c_slab(c_im)], axis=1).astype(BF16)
    avec = jnp.concatenate([abar_re.reshape(S5_SLABS, S5_SLAB_STATE),
                            abar_im.reshape(S5_SLABS, S5_SLAB_STATE)], axis=1)
    avec = jnp.broadcast_to(avec[:, None, :], (S5_SLABS, BATCH, 2 * S5_SLAB_STATE))
    return bw, avec, cw


def _suffix_matrix(n):
    j = lax.broadcasted_iota(jnp.int32, (n, n), 0)
    s = lax.broadcasted_iota(jnp.int32, (n, n), 1)
    t = (j > s).astype(BF16)
    return jnp.concatenate([t, t], axis=0)


def kernel(x, meta_tokens, norm_mix, norm_ffn, s5_a_re, s5_a_im, s5_log_dt, s5_b_re, s5_b_im,
           s5_c_re, s5_c_im, s5_d, s5_w_glu, norm_kv, w_kv, w_q, w_o, w_ffn_in, w_ffn_out,
           norm_final):
    hd = N_HEADS * HEAD_DIM
    row = lambda g: g.reshape(1, D_MODEL)
    bw, avec, cw = _s5_weights(s5_a_re[0], s5_a_im[0], s5_log_dt[0], s5_b_re[0], s5_b_im[0],
                               s5_c_re[0], s5_c_im[0])
    wglu = s5_w_glu[0].astype(BF16)
    wg = [w_ffn_in[i, :, :D_FF].astype(BF16) for i in range(2)]
    wu = [w_ffn_in[i, :, D_FF:].astype(BF16) for i in range(2)]
    wo = [w_ffn_out[i].astype(BF16) for i in range(2)]
    wq = w_q[0].astype(BF16)
    wk = w_kv[:, :hd].astype(BF16)
    wv = w_kv[:, hd:].astype(BF16)
    wo_att = w_o[0].astype(BF16)

    h_real = jnp.transpose(x, (1, 0, 2)).reshape(SEQ * BATCH, D_MODEL)
    h_meta = jnp.repeat(meta_tokens.astype(x.dtype), BATCH, axis=0)

    s5 = functools.partial(_s5_layer, gain=row(norm_mix[0]), bw=bw, avec=avec, cw=cw,
                           d_skip=row(s5_d[0]), wglu=wglu)
    zero_state = jnp.zeros((S5_SLABS, BATCH, 2 * S5_SLAB_STATE), F32)
    h_meta, meta_state = s5(h_meta, init_state=zero_state, chunk=N_META)
    h_real, _ = s5(h_real, init_state=meta_state, chunk=S5_CHUNK)

    ffn0 = functools.partial(_ffn_layer, gain=row(norm_ffn[0]), wg=wg[0], wu=wu[0], wo=wo[0])
    h_meta = ffn0(h_meta, rows_per_step=N_META * BATCH)
    h_real = ffn0(h_real, rows_per_step=FFN_ROWS)

    proj = functools.partial(_proj_layer, gq=row(norm_mix[1]), gkv=row(norm_kv),
                             wq=wq, wk=wk, wv=wv)
    _, k_meta, v_meta = proj(h_meta, rows_per_step=N_META * BATCH)
    q, k, v = proj(h_real, rows_per_step=PROJ_ROWS)

    pad = ((0, META_PAD - N_META), (0, 0))
    k_meta = jnp.pad(k_meta.reshape(N_META, BATCH, hd)[:, 0], pad)
    v_meta = jnp.pad(v_meta.reshape(N_META, BATCH, hd)[:, 0], pad)

    view = lambda a: a.reshape(SEQ, BATCH * a.shape[-1])
    o = _attention(view(q), view(k), view(v), k_meta, v_meta,
                   _suffix_matrix(ATT_K), _suffix_matrix(META_PAD))

    return _out_layer(view(h_real), o, wo_att, row(norm_ffn[1]), wg[1], wu[1], wo[1],
                      row(norm_final), rows_per_step=FFN_ROWS)
```

```python
import functools
import math

import jax
import jax.numpy as jnp
from jax import lax
from jax.experimental import pallas as pl
from jax.experimental.pallas import tpu as pltpu

D_MODEL = 1024
BATCH = 8
SEQ = 4096
N_META = 16
S5_GROUP = 16
S5_GROUPS = 64
S5_STATE = 64
N_HEADS = 16
HEAD_DIM = 64
D_FF = 2816
RMS_EPS = 1e-6

LANES = 128
SUBLANES = 8
MXU_DIM = 256
VMEM_LIMIT_BYTES = 56 * 1024 * 1024

S5_SLAB_GROUPS = MXU_DIM // S5_GROUP
S5_SLABS = S5_GROUPS // S5_SLAB_GROUPS
S5_SLAB_STATE = S5_SLAB_GROUPS * S5_STATE
S5_CHUNK = 64

FFN_ROWS = 512
FFN_F_CHUNKS = ((0, 1024), (1024, 2048), (2048, D_FF))
PROJ_ROWS = 512

ATT_Q = 256
ATT_K = 256
META_PAD = 128
MASKED_LOGIT = -1e30
REM_FLOOR = -104.0

F32 = jnp.float32
BF16 = jnp.bfloat16


def _rmsnorm(x, gain):
    ms = jnp.mean(x * x, axis=-1, keepdims=True)
    return x * lax.rsqrt(ms + RMS_EPS) * gain


def _gelu_tanh(y):
    c = math.sqrt(2.0 / math.pi)
    return 0.5 * y * (1.0 + jnp.tanh(c * (y + 0.044715 * (y * y * y))))


def _sigmoid(x):
    return 1.0 / (1.0 + jnp.exp(-x))


def _const_spec(shape):
    nd = len(shape)
    return pl.BlockSpec(shape, lambda *_: (0,) * nd, pipeline_mode=pl.Buffered(1))


def _s5_kernel(x_ref, gain_ref, bw_ref, avec_ref, cw_ref, d_ref, wglu_ref, init_ref,
               out_ref, fin_ref, state_ref, hs_ref, bu_ref, y_ref, *, chunk):
    half = S5_SLAB_STATE
    lane_slabs = D_MODEL // LANES

    @pl.when(pl.program_id(0) == 0)
    def _():
        state_ref[...] = init_ref[...]

    for b in range(BATCH):
        for j in range(lane_slabs):
            hs_ref[j, pl.ds(b, chunk, stride=BATCH), :] = x_ref[b, :, j * LANES:(j + 1) * LANES]
    h = jnp.concatenate([hs_ref[j] for j in range(lane_slabs)], axis=1)
    u = _rmsnorm(h, gain_ref[...])
    ub = u.astype(BF16)
    for s in range(S5_SLABS):
        bu_ref[...] = jnp.dot(ub[:, s * MXU_DIM:(s + 1) * MXU_DIM], bw_ref[s],
                              preferred_element_type=F32)
        a_re = avec_ref[s, :, :half]
        a_im = avec_ref[s, :, half:]

        def step(l, carry, a_re=a_re, a_im=a_im):
            x_re, x_im = carry
            row = pl.multiple_of(l * SUBLANES, SUBLANES)
            b_re = bu_ref[pl.ds(row, SUBLANES), :half]
            b_im = bu_ref[pl.ds(row, SUBLANES), half:]
            n_re = a_re * x_re - a_im * x_im + b_re
            n_im = a_re * x_im + a_im * x_re + b_im
            bu_ref[pl.ds(row, SUBLANES), :half] = n_re
            bu_ref[pl.ds(row, SUBLANES), half:] = n_im
            return n_re, n_im

        x_re, x_im = lax.fori_loop(0, chunk, step,
                                   (state_ref[s, :, :half], state_ref[s, :, half:]), unroll=2)
        state_ref[s, :, :half] = x_re
        state_ref[s, :, half:] = x_im
        y_ref[:, s * MXU_DIM:(s + 1) * MXU_DIM] = jnp.dot(
            bu_ref[...].astype(BF16), cw_ref[s], preferred_element_type=F32)

    y = y_ref[...] + d_ref[...] * u
    z = _gelu_tanh(y).astype(BF16)
    vg = jnp.dot(z, wglu_ref[...], preferred_element_type=F32)
    res = h + vg[:, :D_MODEL] * _sigmoid(vg[:, D_MODEL:])
    for j in range(lane_slabs):
        hs_ref[j] = res[:, j * LANES:(j + 1) * LANES]
    for b in range(BATCH):
        for j in range(lane_slabs):
            out_ref[b, :, j * LANES:(j + 1) * LANES] = hs_ref[j, pl.ds(b, chunk, stride=BATCH), :]

    @pl.when(pl.program_id(0) == pl.num_programs(0) - 1)
    def _():
        fin_ref[...] = state_ref[...]


def _s5_layer(x, gain, bw, avec, cw, d_skip, wglu, init_state, *, chunk):
    length = x.shape[1]
    m = chunk * BATCH
    assert length % chunk == 0
    state_shape = (S5_SLABS, BATCH, 2 * S5_SLAB_STATE)
    io_spec = pl.BlockSpec((BATCH, chunk, D_MODEL), lambda i: (0, i, 0))
    return pl.pallas_call(
        functools.partial(_s5_kernel, chunk=chunk),
        grid=(length // chunk,),
        in_specs=[
            io_spec,
            _const_spec((1, D_MODEL)),
            _const_spec(bw.shape),
            _const_spec(avec.shape),
            _const_spec(cw.shape),
            _const_spec((1, D_MODEL)),
            _const_spec(wglu.shape),
            _const_spec(state_shape),
        ],
        out_specs=[
            io_spec,
            pl.BlockSpec(state_shape, lambda i: (0, 0, 0)),
        ],
        out_shape=[
            jax.ShapeDtypeStruct(x.shape, F32),
            jax.ShapeDtypeStruct(state_shape, F32),
        ],
        scratch_shapes=[
            pltpu.VMEM(state_shape, F32),
            pltpu.VMEM((D_MODEL // LANES, m, LANES), F32),
            pltpu.VMEM((m, 2 * S5_SLAB_STATE), F32),
            pltpu.VMEM((m, D_MODEL), F32),
        ],
        compiler_params=pltpu.CompilerParams(
            dimension_semantics=("arbitrary",), vmem_limit_bytes=VMEM_LIMIT_BYTES),
        name="s5_layer",
    )(x, gain, bw, avec, cw, d_skip, wglu, init_state)


def _swiglu(hn, wg_ref, wu_ref, wo_ref):
    acc = None
    for f0, f1 in FFN_F_CHUNKS:
        g = jnp.dot(hn, wg_ref[:, f0:f1], preferred_element_type=F32)
        u = jnp.dot(hn, wu_ref[:, f0:f1], preferred_element_type=F32)
        a = (g * _sigmoid(g) * u).astype(BF16)
        part = jnp.dot(a, wo_ref[f0:f1, :], preferred_element_type=F32)
        acc = part if acc is None else acc + part
    return acc


def _ffn_kernel(h_ref, gain_ref, wg_ref, wu_ref, wo_ref, out_ref):
    h = h_ref[...]
    hn = _rmsnorm(h, gain_ref[...]).astype(BF16)
    out_ref[...] = h + _swiglu(hn, wg_ref, wu_ref, wo_ref)


def _ffn_layer(h, gain, wg, wu, wo, *, rows_per_step):
    rows = h.shape[0]
    assert rows % rows_per_step == 0
    return pl.pallas_call(
        _ffn_kernel,
        grid=(rows // rows_per_step,),
        in_specs=[
            pl.BlockSpec((rows_per_step, D_MODEL), lambda i: (i, 0)),
            _const_spec((1, D_MODEL)),
            _const_spec(wg.shape),
            _const_spec(wu.shape),
            _const_spec(wo.shape),
        ],
        out_specs=pl.BlockSpec((rows_per_step, D_MODEL), lambda i: (i, 0)),
        out_shape=jax.ShapeDtypeStruct((rows, D_MODEL), F32),
        compiler_params=pltpu.CompilerParams(
            dimension_semantics=("parallel",), vmem_limit_bytes=VMEM_LIMIT_BYTES),
        name="ffn_layer",
    )(h, gain, wg, wu, wo)


def _proj_kernel(h_ref, gq_ref, gkv_ref, wq_ref, wk_ref, wv_ref, q_ref, k_ref, v_ref):
    h = h_ref[...]
    y = h * lax.rsqrt(jnp.mean(h * h, axis=-1, keepdims=True) + RMS_EPS)
    hq = (y * gq_ref[...]).astype(BF16)
    hkv = (y * gkv_ref[...]).astype(BF16)
    scale = 1.0 / math.sqrt(HEAD_DIM)
    q_ref[...] = (jnp.dot(hq, wq_ref[...], preferred_element_type=F32) * scale).astype(BF16)
    k_ref[...] = jnp.dot(hkv, wk_ref[...], preferred_element_type=F32).astype(BF16)
    v_ref[...] = jnp.dot(hkv, wv_ref[...], preferred_element_type=F32).astype(BF16)


def _proj_layer(h, gq, gkv, wq, wk, wv, *, rows_per_step):
    rows = h.shape[0]
    assert rows % rows_per_step == 0
    hd = N_HEADS * HEAD_DIM
    row_spec = pl.BlockSpec((rows_per_step, hd), lambda i: (i, 0))
    return pl.pallas_call(
        _proj_kernel,
        grid=(rows // rows_per_step,),
        in_specs=[
            pl.BlockSpec((rows_per_step, D_MODEL), lambda i: (i, 0)),
            _const_spec((1, D_MODEL)),
            _const_spec((1, D_MODEL)),
            _const_spec(wq.shape),
            _const_spec(wk.shape),
            _const_spec(wv.shape),
        ],
        out_specs=[row_spec, row_spec, row_spec],
        out_shape=[jax.ShapeDtypeStruct((rows, hd), BF16)] * 3,
        compiler_params=pltpu.CompilerParams(
            dimension_semantics=("parallel",), vmem_limit_bytes=VMEM_LIMIT_BYTES),
        name="qkv_proj",
    )(h, gq, gkv, wq, wk, wv)


def _attn_kernel(q_ref, k_ref, v_ref, km_ref, vm_ref, tri_ref, trim_ref, dbias_ref, mbias_ref,
                 o_ref, acc_ref, rem_ref):
    lane = lax.broadcasted_iota(jnp.int32, (1, LANES), 1)
    head0 = lane < HEAD_DIM

    def block(q_heads, kb, vb, tri_neg, bias):
        nk = kb.shape[0]
        for hd in range(2):
            z = lax.dot_general(q_heads[hd], kb, (((1,), (1,)), ((), ())),
                                preferred_element_type=F32)
            if bias is not None:
                z = z + bias
            sp = jnp.maximum(z, 0.0) + jnp.log(1.0 + jnp.exp(-jnp.abs(z)))
            hi = sp.astype(BF16)
            lo = (sp - hi.astype(F32)).astype(BF16)
            suffix = jnp.dot(jnp.concatenate([hi, lo], axis=1), tri_neg,
                             preferred_element_type=F32)
            rem = rem_ref[hd]
            rem_b = jnp.concatenate([rem] * (nk // LANES), axis=1)
            w = jnp.exp((z - sp) + suffix + rem_b)
            acc_ref[hd] += jnp.dot(w.astype(BF16), vb, preferred_element_type=F32)
            rem_ref[hd] = rem - jnp.sum(sp, axis=1, keepdims=True)

    def q_tile(qi, carry):
        q0 = pl.multiple_of(qi * ATT_Q, ATT_Q)
        q = q_ref[pl.ds(q0, ATT_Q), :]
        zero = jnp.zeros_like(q)
        q_heads = (jnp.where(head0, q, zero), jnp.where(head0, zero, q))
        acc_ref[...] = jnp.zeros_like(acc_ref)
        rem_ref[...] = jnp.zeros_like(rem_ref)

        block(q_heads, k_ref[pl.ds(q0, ATT_K), :], v_ref[pl.ds(q0, ATT_K), :],
              tri_ref[...], dbias_ref[...])

        def earlier(c):
            j, _ = c
            k0 = pl.multiple_of(j * ATT_K, ATT_K)
            block(q_heads, k_ref[pl.ds(k0, ATT_K), :], v_ref[pl.ds(k0, ATT_K), :],
                  tri_ref[...], None)
            live = (jnp.max(rem_ref[...]) >= REM_FLOOR).astype(jnp.int32)
            return j - 1, live

        _, live = lax.while_loop(lambda c: jnp.logical_and(c[0] >= 0, c[1] > 0), earlier,
                                 (qi - 1, jnp.int32(1)))

        @pl.when(live > 0)
        def _():
            block(q_heads, km_ref[...], vm_ref[...], trim_ref[...], mbias_ref[...])

        o_ref[pl.ds(q0, ATT_Q), :] = jnp.where(head0, acc_ref[0], acc_ref[1]).astype(o_ref.dtype)
        return carry

    lax.fori_loop(0, SEQ // ATT_Q, q_tile, 0)


def _attention(q, k, v, k_meta, v_meta):
    pairs = N_HEADS * HEAD_DIM // LANES
    seq_spec = pl.BlockSpec((None, SEQ, LANES), lambda b, p: (b, 0, p))
    meta_spec = pl.BlockSpec((META_PAD, LANES), lambda b, p: (0, p))
    tri = _suffix_matrix(ATT_K)
    tri_meta = _suffix_matrix(META_PAD)
    row = lax.broadcasted_iota(jnp.int32, (ATT_Q, ATT_K), 0)
    col = lax.broadcasted_iota(jnp.int32, (ATT_Q, ATT_K), 1)
    diag_bias = jnp.where(col < row, 0.0, MASKED_LOGIT).astype(F32)
    mcol = lax.broadcasted_iota(jnp.int32, (1, META_PAD), 1)
    meta_bias = jnp.where(mcol < N_META, 0.0, MASKED_LOGIT).astype(F32)
    return pl.pallas_call(
        _attn_kernel,
        grid=(BATCH, pairs),
        in_specs=[
            seq_spec, seq_spec, seq_spec, meta_spec, meta_spec,
            _const_spec(tri.shape), _const_spec(tri_meta.shape),
            _const_spec(diag_bias.shape), _const_spec(meta_bias.shape),
        ],
        out_specs=seq_spec,
        out_shape=jax.ShapeDtypeStruct(q.shape, BF16),
        scratch_shapes=[
            pltpu.VMEM((2, ATT_Q, LANES), F32),
            pltpu.VMEM((2, ATT_Q, LANES), F32),
        ],
        compiler_params=pltpu.CompilerParams(
            dimension_semantics=("parallel", "parallel"), vmem_limit_bytes=VMEM_LIMIT_BYTES),
        name="stickbreak_attn",
    )(q, k, v, k_meta, v_meta, tri, tri_meta, diag_bias, meta_bias)


def _out_kernel(h_ref, o_ref, wo_att_ref, gain_ref, wg_ref, wu_ref, wo_ref, gfin_ref, out_ref):
    h = h_ref[...] + jnp.dot(o_ref[...], wo_att_ref[...], preferred_element_type=F32)
    hn = _rmsnorm(h, gain_ref[...]).astype(BF16)
    h = h + _swiglu(hn, wg_ref, wu_ref, wo_ref)
    out_ref[...] = _rmsnorm(h, gfin_ref[...])


def _out_layer(h, o, wo_att, gain, wg, wu, wo, gfin, *, rows_per_step):
    rows = h.shape[0]
    assert rows % rows_per_step == 0
    row_spec = pl.BlockSpec((rows_per_step, D_MODEL), lambda i: (i, 0))
    return pl.pallas_call(
        _out_kernel,
        grid=(rows // rows_per_step,),
        in_specs=[
            row_spec,
            pl.BlockSpec((rows_per_step, o.shape[1]), lambda i: (i, 0)),
            _const_spec(wo_att.shape),
            _const_spec((1, D_MODEL)),
            _const_spec(wg.shape),
            _const_spec(wu.shape),
            _const_spec(wo.shape),
            _const_spec((1, D_MODEL)),
        ],
        out_specs=row_spec,
        out_shape=jax.ShapeDtypeStruct((rows, D_MODEL), F32),
        compiler_params=pltpu.CompilerParams(
            dimension_semantics=("parallel",), vmem_limit_bytes=VMEM_LIMIT_BYTES),
        name="oproj_ffn_final",
    )(h, o, wo_att, gain, wg, wu, wo, gfin)


def _s5_weights(a_re, a_im, log_dt, b_re, b_im, c_re, c_im):
    dt = jnp.exp(log_dt)[:, None]
    mag = jnp.exp(dt * a_re)
    ang = dt * a_im
    abar_re = mag * jnp.cos(ang)
    abar_im = mag * jnp.sin(ang)
    den = a_re * a_re + a_im * a_im
    coef_re = ((abar_re - 1.0) * a_re + abar_im * a_im) / den
    coef_im = (abar_im * a_re - (abar_re - 1.0) * a_im) / den
    bbar_re = coef_re[..., None] * b_re - coef_im[..., None] * b_im
    bbar_im = coef_re[..., None] * b_im + coef_im[..., None] * b_re
    eye = jnp.eye(S5_SLAB_GROUPS, dtype=F32)
    slab = (S5_SLABS, S5_SLAB_GROUPS)

    def b_slab(bb):
        w = jnp.einsum('sgpc,gh->sgchp', bb.reshape(slab + (S5_STATE, S5_GROUP)), eye)
        return w.reshape(S5_SLABS, MXU_DIM, S5_SLAB_STATE)

    def c_slab(cc):
        w = jnp.einsum('sgcp,gh->sgphc', cc.reshape(slab + (S5_GROUP, S5_STATE)), eye)
        return w.reshape(S5_SLABS, S5_SLAB_STATE, MXU_DIM)

    bw = jnp.concatenate([b_slab(bbar_re), b_slab(bbar_im)], axis=2).astype(BF16)
    cw = jnp.concatenate([c_slab(c_re), -c_slab(c_im)], axis=1).astype(BF16)
    avec = jnp.concatenate([abar_re.reshape(S5_SLABS, S5_SLAB_STATE),
                            abar_im.reshape(S5_SLABS, S5_SLAB_STATE)], axis=1)
    avec = jnp.broadcast_to(avec[:, None, :], (S5_SLABS, BATCH, 2 * S5_SLAB_STATE))
    return bw, avec, cw


def _suffix_matrix(n):
    j = lax.broadcasted_iota(jnp.int32, (n, n), 0)
    s = lax.broadcasted_iota(jnp.int32, (n, n), 1)
    t = jnp.where(j > s, -1.0, 0.0).astype(BF16)
    return jnp.concatenate([t, t], axis=0)


def kernel(x, meta_tokens, norm_mix, norm_ffn, s5_a_re, s5_a_im, s5_log_dt, s5_b_re, s5_b_im,
           s5_c_re, s5_c_im, s5_d, s5_w_glu, norm_kv, w_kv, w_q, w_o, w_ffn_in, w_ffn_out,
           norm_final):
    hd = N_HEADS * HEAD_DIM
    row = lambda g: g.reshape(1, D_MODEL)
    bw, avec, cw = _s5_weights(s5_a_re[0], s5_a_im[0], s5_log_dt[0], s5_b_re[0], s5_b_im[0],
                               s5_c_re[0], s5_c_im[0])
    wglu = s5_w_glu[0].astype(BF16)
    wg = [w_ffn_in[i, :, :D_FF].astype(BF16) for i in range(2)]
    wu = [w_ffn_in[i, :, D_FF:].astype(BF16) for i in range(2)]
    wo = [w_ffn_out[i].astype(BF16) for i in range(2)]
    wq = w_q[0].astype(BF16)
    wk = w_kv[:, :hd].astype(BF16)
    wv = w_kv[:, hd:].astype(BF16)
    wo_att = w_o[0].astype(BF16)

    s5 = functools.partial(_s5_layer, gain=row(norm_mix[0]), bw=bw, avec=avec, cw=cw,
                           d_skip=row(s5_d[0]), wglu=wglu)
    x_meta = jnp.broadcast_to(meta_tokens.astype(x.dtype)[None], (BATCH, N_META, D_MODEL))
    zero_state = jnp.zeros((S5_SLABS, BATCH, 2 * S5_SLAB_STATE), F32)
    h_meta, meta_state = s5(x_meta, init_state=zero_state, chunk=N_META)
    h_real, _ = s5(x, init_state=meta_state, chunk=S5_CHUNK)
    h_meta = h_meta[0]
    h_real = h_real.reshape(BATCH * SEQ, D_MODEL)

    ffn0 = functools.partial(_ffn_layer, gain=row(norm_ffn[0]), wg=wg[0], wu=wu[0], wo=wo[0])
    h_meta = ffn0(h_meta, rows_per_step=N_META)
    h_real = ffn0(h_real, rows_per_step=FFN_ROWS)

    proj = functools.partial(_proj_layer, gq=row(norm_mix[1]), gkv=row(norm_kv),
                             wq=wq, wk=wk, wv=wv)
    _, k_meta, v_meta = proj(h_meta, rows_per_step=N_META)
    q, k, v = proj(h_real, rows_per_step=PROJ_ROWS)

    pad = ((0, META_PAD - N_META), (0, 0))
    seq = lambda a: a.reshape(BATCH, SEQ, hd)
    o = _attention(seq(q), seq(k), seq(v), jnp.pad(k_meta, pad), jnp.pad(v_meta, pad))

    out = _out_layer(h_real, o.reshape(BATCH * SEQ, hd), wo_att, row(norm_ffn[1]),
                     wg[1], wu[1], wo[1], row(norm_final), rows_per_step=FFN_ROWS)
    return out.reshape(BATCH, SEQ, D_MODEL)
```

```python
import functools
import math

import jax
import jax.numpy as jnp
from jax import lax
from jax.experimental import pallas as pl
from jax.experimental.pallas import tpu as pltpu

D_MODEL = 1024
BATCH = 8
SEQ = 4096
N_META = 16
S5_GROUP = 16
S5_GROUPS = 64
S5_STATE = 64
N_HEADS = 16
HEAD_DIM = 64
D_FF = 2816
RMS_EPS = 1e-6

LANES = 128
SUBLANES = 8
MXU_DIM = 256
VMEM_LIMIT_BYTES = 56 * 1024 * 1024

S5_SLAB_GROUPS = MXU_DIM // S5_GROUP
S5_SLABS = S5_GROUPS // S5_SLAB_GROUPS
S5_SLAB_STATE = S5_SLAB_GROUPS * S5_STATE
S5_CHUNK = 64

FFN_ROWS = 512
FFN_F_CHUNKS = ((0, 1024), (1024, 2048), (2048, D_FF))
PROJ_ROWS = 512

ATT_Q = 256
ATT_K = 256
META_PAD = 128
MASKED_LOGIT = -1e30
REM_FLOOR = -151.0
LOG2E = 1.4426950408889634

F32 = jnp.float32
BF16 = jnp.bfloat16


def _rmsnorm(x, gain):
    ms = jnp.mean(x * x, axis=-1, keepdims=True)
    return x * lax.rsqrt(ms + RMS_EPS) * gain


def _gelu_tanh(y):
    c = math.sqrt(2.0 / math.pi)
    return 0.5 * y * (1.0 + jnp.tanh(c * (y + 0.044715 * (y * y * y))))


def _sigmoid(x):
    return 1.0 / (1.0 + jnp.exp(-x))


def _const_spec(shape):
    nd = len(shape)
    return pl.BlockSpec(shape, lambda *_: (0,) * nd, pipeline_mode=pl.Buffered(1))


def _s5_kernel(x_ref, gain_ref, bw_ref, avec_ref, cw_ref, d_ref, wglu_ref, init_ref,
               out_ref, fin_ref, state_ref, hs_ref, bu_ref, y_ref, *, chunk):
    half = S5_SLAB_STATE
    lane_slabs = D_MODEL // LANES

    @pl.when(pl.program_id(0) == 0)
    def _():
        state_ref[...] = init_ref[...]

    for b in range(BATCH):
        for j in range(lane_slabs):
            hs_ref[j, pl.ds(b, chunk, stride=BATCH), :] = x_ref[b, :, j * LANES:(j + 1) * LANES]
    h = jnp.concatenate([hs_ref[j] for j in range(lane_slabs)], axis=1)
    u = _rmsnorm(h, gain_ref[...])
    ub = u.astype(BF16)
    for s in range(S5_SLABS):
        bu_ref[...] = jnp.dot(ub[:, s * MXU_DIM:(s + 1) * MXU_DIM], bw_ref[s],
                              preferred_element_type=F32)
        a_re = avec_ref[s, :, :half]
        a_im = avec_ref[s, :, half:]

        def step(l, carry, a_re=a_re, a_im=a_im):
            x_re, x_im = carry
            row = pl.multiple_of(l * SUBLANES, SUBLANES)
            b_re = bu_ref[pl.ds(row, SUBLANES), :half]
            b_im = bu_ref[pl.ds(row, SUBLANES), half:]
            n_re = a_re * x_re - a_im * x_im + b_re
            n_im = a_re * x_im + a_im * x_re + b_im
            bu_ref[pl.ds(row, SUBLANES), :half] = n_re
            bu_ref[pl.ds(row, SUBLANES), half:] = n_im
            return n_re, n_im

        x_re, x_im = lax.fori_loop(0, chunk, step,
                                   (state_ref[s, :, :half], state_ref[s, :, half:]), unroll=True)
        state_ref[s, :, :half] = x_re
        state_ref[s, :, half:] = x_im
        y_ref[:, s * MXU_DIM:(s + 1) * MXU_DIM] = jnp.dot(
            bu_ref[...].astype(BF16), cw_ref[s], preferred_element_type=F32)

    y = y_ref[...] + d_ref[...] * u
    z = _gelu_tanh(y).astype(BF16)
    vg = jnp.dot(z, wglu_ref[...], preferred_element_type=F32)
    res = h + vg[:, :D_MODEL] * _sigmoid(vg[:, D_MODEL:])
    for j in range(lane_slabs):
        hs_ref[j] = res[:, j * LANES:(j + 1) * LANES]
    for b in range(BATCH):
        for j in range(lane_slabs):
            out_ref[b, :, j * LANES:(j + 1) * LANES] = hs_ref[j, pl.ds(b, chunk, stride=BATCH), :]

    @pl.when(pl.program_id(0) == pl.num_programs(0) - 1)
    def _():
        fin_ref[...] = state_ref[...]


def _s5_layer(x, gain, bw, avec, cw, d_skip, wglu, init_state, *, chunk):
    length = x.shape[1]
    m = chunk * BATCH
    assert length % chunk == 0
    state_shape = (S5_SLABS, BATCH, 2 * S5_SLAB_STATE)
    io_spec = pl.BlockSpec((BATCH, chunk, D_MODEL), lambda i: (0, i, 0))
    return pl.pallas_call(
        functools.partial(_s5_kernel, chunk=chunk),
        grid=(length // chunk,),
        in_specs=[
            io_spec,
            _const_spec((1, D_MODEL)),
            _const_spec(bw.shape),
            _const_spec(avec.shape),
            _const_spec(cw.shape),
            _const_spec((1, D_MODEL)),
            _const_spec(wglu.shape),
            _const_spec(state_shape),
        ],
        out_specs=[
            io_spec,
            pl.BlockSpec(state_shape, lambda i: (0, 0, 0)),
        ],
        out_shape=[
            jax.ShapeDtypeStruct(x.shape, F32),
            jax.ShapeDtypeStruct(state_shape, F32),
        ],
        scratch_shapes=[
            pltpu.VMEM(state_shape, F32),
            pltpu.VMEM((D_MODEL // LANES, m, LANES), F32),
            pltpu.VMEM((m, 2 * S5_SLAB_STATE), F32),
            pltpu.VMEM((m, D_MODEL), F32),
        ],
        compiler_params=pltpu.CompilerParams(
            dimension_semantics=("arbitrary",), vmem_limit_bytes=VMEM_LIMIT_BYTES),
        name="s5_layer",
    )(x, gain, bw, avec, cw, d_skip, wglu, init_state)


def _swiglu(hn, wg_ref, wu_ref, wo_ref):
    acc = None
    for f0, f1 in FFN_F_CHUNKS:
        g = jnp.dot(hn, wg_ref[:, f0:f1], preferred_element_type=F32)
        u = jnp.dot(hn, wu_ref[:, f0:f1], preferred_element_type=F32)
        a = (g * _sigmoid(g) * u).astype(BF16)
        part = jnp.dot(a, wo_ref[f0:f1, :], preferred_element_type=F32)
        acc = part if acc is None else acc + part
    return acc


def _ffn_kernel(h_ref, gain_ref, wg_ref, wu_ref, wo_ref, out_ref):
    h = h_ref[...]
    hn = _rmsnorm(h, gain_ref[...]).astype(BF16)
    out_ref[...] = h + _swiglu(hn, wg_ref, wu_ref, wo_ref)


def _ffn_layer(h, gain, wg, wu, wo, *, rows_per_step):
    rows = h.shape[0]
    assert rows % rows_per_step == 0
    return pl.pallas_call(
        _ffn_kernel,
        grid=(rows // rows_per_step,),
        in_specs=[
            pl.BlockSpec((rows_per_step, D_MODEL), lambda i: (i, 0)),
            _const_spec((1, D_MODEL)),
            _const_spec(wg.shape),
            _const_spec(wu.shape),
            _const_spec(wo.shape),
        ],
        out_specs=pl.BlockSpec((rows_per_step, D_MODEL), lambda i: (i, 0)),
        out_shape=jax.ShapeDtypeStruct((rows, D_MODEL), F32),
        compiler_params=pltpu.CompilerParams(
            dimension_semantics=("parallel",), vmem_limit_bytes=VMEM_LIMIT_BYTES),
        name="ffn_layer",
    )(h, gain, wg, wu, wo)


def _proj_kernel(h_ref, gq_ref, gkv_ref, wq_ref, wk_ref, wv_ref, q_ref, k_ref, v_ref):
    h = h_ref[...]
    y = h * lax.rsqrt(jnp.mean(h * h, axis=-1, keepdims=True) + RMS_EPS)
    hq = (y * gq_ref[...]).astype(BF16)
    hkv = (y * gkv_ref[...]).astype(BF16)
    scale = LOG2E / math.sqrt(HEAD_DIM)
    q_ref[...] = (jnp.dot(hq, wq_ref[...], preferred_element_type=F32) * scale).astype(BF16)
    k_ref[...] = jnp.dot(hkv, wk_ref[...], preferred_element_type=F32).astype(BF16)
    v_ref[...] = jnp.dot(hkv, wv_ref[...], preferred_element_type=F32).astype(BF16)


def _proj_layer(h, gq, gkv, wq, wk, wv, *, rows_per_step):
    rows = h.shape[0]
    assert rows % rows_per_step == 0
    hd = N_HEADS * HEAD_DIM
    row_spec = pl.BlockSpec((rows_per_step, hd), lambda i: (i, 0))
    return pl.pallas_call(
        _proj_kernel,
        grid=(rows // rows_per_step,),
        in_specs=[
            pl.BlockSpec((rows_per_step, D_MODEL), lambda i: (i, 0)),
            _const_spec((1, D_MODEL)),
            _const_spec((1, D_MODEL)),
            _const_spec(wq.shape),
            _const_spec(wk.shape),
            _const_spec(wv.shape),
        ],
        out_specs=[row_spec, row_spec, row_spec],
        out_shape=[jax.ShapeDtypeStruct((rows, hd), BF16)] * 3,
        compiler_params=pltpu.CompilerParams(
            dimension_semantics=("parallel",), vmem_limit_bytes=VMEM_LIMIT_BYTES),
        name="qkv_proj",
    )(h, gq, gkv, wq, wk, wv)


def _attn_kernel(q_ref, k_ref, v_ref, km_ref, vm_ref, tri_ref, trim_ref, dbias_ref, mbias_ref,
                 o_ref, acc_ref, rem_ref):
    lane = lax.broadcasted_iota(jnp.int32, (1, LANES), 1)
    head0 = lane < HEAD_DIM

    def block(t, q_st, kb, vb, tri_neg, bias):
        nk = kb.shape[0]
        z = lax.dot_general(q_st, kb, (((1,), (1,)), ((), ())),
                            preferred_element_type=F32)
        if bias is not None:
            z = z + bias
        sp = jnp.maximum(z, 0.0) + jnp.log2(1.0 + jnp.exp2(-jnp.abs(z)))
        suffix = jnp.dot(sp.astype(BF16), tri_neg, preferred_element_type=F32)
        rem = rem_ref[t]
        rem_b = jnp.concatenate([rem] * (nk // LANES), axis=1)
        w = jnp.exp2((z - sp) + suffix + rem_b)
        acc_ref[t] += jnp.dot(w.astype(BF16), vb, preferred_element_type=F32)
        rem_ref[t] = rem - jnp.sum(sp, axis=1, keepdims=True)

    def kv_block(j):
        k0 = pl.multiple_of(j * ATT_K, ATT_K)
        return k_ref[pl.ds(k0, ATT_K), :], v_ref[pl.ds(k0, ATT_K), :]

    def stacked_q(qi):
        q = q_ref[pl.ds(pl.multiple_of(qi * ATT_Q, ATT_Q), ATT_Q), :]
        zero = jnp.zeros_like(q)
        return jnp.concatenate([jnp.where(head0, q, zero), jnp.where(head0, zero, q)], axis=0)

    def live_flag(t):
        return (jnp.max(rem_ref[t]) >= REM_FLOOR).astype(jnp.int32)

    def finish(t, qi, q_st, live0, j0):
        def earlier(c):
            j, _ = c
            block(t, q_st, *kv_block(j), tri_ref[...], None)
            return j - 1, live_flag(t)

        _, live = lax.while_loop(lambda c: jnp.logical_and(c[0] >= 0, c[1] > 0), earlier,
                                 (j0, live0))

        @pl.when(live > 0)
        def _():
            block(t, q_st, km_ref[...], vm_ref[...], trim_ref[...], mbias_ref[...])

        o_ref[pl.ds(pl.multiple_of(qi * ATT_Q, ATT_Q), ATT_Q), :] = jnp.where(
            head0, acc_ref[t, :ATT_Q], acc_ref[t, ATT_Q:]).astype(o_ref.dtype)

    def tile_pair(qa, first):
        qb = qa + 1
        q_a, q_b = stacked_q(qa), stacked_q(qb)
        acc_ref[...] = jnp.zeros_like(acc_ref)
        rem_ref[...] = jnp.zeros_like(rem_ref)
        k_a, v_a = kv_block(qa)
        k_b, v_b = kv_block(qb)
        block(0, q_a, k_a, v_a, tri_ref[...], dbias_ref[...])
        block(1, q_b, k_b, v_b, tri_ref[...], dbias_ref[...])
        block(1, q_b, k_a, v_a, tri_ref[...], None)
        if first:
            live_a = jnp.int32(1)
        else:
            block(0, q_a, *kv_block(qa - 1), tri_ref[...], None)
            live_a = live_flag(0)
        live_b = live_flag(1)
        finish(0, qa, q_a, live_a, qa - 2)
        finish(1, qb, q_b, live_b, qb - 2)

    tile_pair(0, True)

    def later_pair(i, carry):
        tile_pair(2 * i, False)
        return carry

    lax.fori_loop(1, SEQ // (2 * ATT_Q), later_pair, 0)


def _attention(q, k, v, k_meta, v_meta):
    pairs = N_HEADS * HEAD_DIM // LANES
    seq_spec = pl.BlockSpec((None, SEQ, LANES), lambda b, p: (b, 0, p))
    meta_spec = pl.BlockSpec((META_PAD, LANES), lambda b, p: (0, p))
    tri = _suffix_matrix(ATT_K)
    tri_meta = _suffix_matrix(META_PAD)
    row = lax.broadcasted_iota(jnp.int32, (ATT_Q, ATT_K), 0)
    col = lax.broadcasted_iota(jnp.int32, (ATT_Q, ATT_K), 1)
    diag_bias = jnp.where(col < row, 0.0, MASKED_LOGIT).astype(F32)
    diag_bias = jnp.concatenate([diag_bias, diag_bias], axis=0)
    mcol = lax.broadcasted_iota(jnp.int32, (1, META_PAD), 1)
    meta_bias = jnp.where(mcol < N_META, 0.0, MASKED_LOGIT).astype(F32)
    return pl.pallas_call(
        _attn_kernel,
        grid=(BATCH, pairs),
        in_specs=[
            seq_spec, seq_spec, seq_spec, meta_spec, meta_spec,
            _const_spec(tri.shape), _const_spec(tri_meta.shape),
            _const_spec(diag_bias.shape), _const_spec(meta_bias.shape),
        ],
        out_specs=seq_spec,
        out_shape=jax.ShapeDtypeStruct(q.shape, BF16),
        scratch_shapes=[
            pltpu.VMEM((2, 2 * ATT_Q, LANES), F32),
            pltpu.VMEM((2, 2 * ATT_Q, LANES), F32),
        ],
        compiler_params=pltpu.CompilerParams(
            dimension_semantics=("parallel", "parallel"), vmem_limit_bytes=VMEM_LIMIT_BYTES),
        name="stickbreak_attn",
    )(q, k, v, k_meta, v_meta, tri, tri_meta, diag_bias, meta_bias)


def _out_kernel(h_ref, o_ref, wo_att_ref, gain_ref, wg_ref, wu_ref, wo_ref, gfin_ref, out_ref):
    h = h_ref[...] + jnp.dot(o_ref[...], wo_att_ref[...], preferred_element_type=F32)
    hn = _rmsnorm(h, gain_ref[...]).astype(BF16)
    h = h + _swiglu(hn, wg_ref, wu_ref, wo_ref)
    out_ref[...] = _rmsnorm(h, gfin_ref[...])


def _out_layer(h, o, wo_att, gain, wg, wu, wo, gfin, *, rows_per_step):
    rows = h.shape[0]
    assert rows % rows_per_step == 0
    row_spec = pl.BlockSpec((rows_per_step, D_MODEL), lambda i: (i, 0))
    return pl.pallas_call(
        _out_kernel,
        grid=(rows // rows_per_step,),
        in_specs=[
            row_spec,
            pl.BlockSpec((rows_per_step, o.shape[1]), lambda i: (i, 0)),
            _const_spec(wo_att.shape),
            _const_spec((1, D_MODEL)),
            _const_spec(wg.shape),
            _const_spec(wu.shape),
            _const_spec(wo.shape),
            _const_spec((1, D_MODEL)),
        ],
        out_specs=row_spec,
        out_shape=jax.ShapeDtypeStruct((rows, D_MODEL), F32),
        compiler_params=pltpu.CompilerParams(
            dimension_semantics=("parallel",), vmem_limit_bytes=VMEM_LIMIT_BYTES),
        name="oproj_ffn_final",
    )(h, o, wo_att, gain, wg, wu, wo, gfin)


def _s5_weights(a_re, a_im, log_dt, b_re, b_im, c_re, c_im):
    dt = jnp.exp(log_dt)[:, None]
    mag = jnp.exp(dt * a_re)
    ang = dt * a_im
    abar_re = mag * jnp.cos(ang)
    abar_im = mag * jnp.sin(ang)
    den = a_re * a_re + a_im * a_im
    coef_re = ((abar_re - 1.0) * a_re + abar_im * a_im) / den
    coef_im = (abar_im * a_re - (abar_re - 1.0) * a_im) / den
    bbar_re = coef_re[..., None] * b_re - coef_im[..., None] * b_im
    bbar_im = coef_re[..., None] * b_im + coef_im[..., None] * b_re
    eye = jnp.eye(S5_SLAB_GROUPS, dtype=F32)
    slab = (S5_SLABS, S5_SLAB_GROUPS)

    def b_slab(bb):
        w = jnp.einsum('sgpc,gh->sgchp', bb.reshape(slab + (S5_STATE, S5_GROUP)), eye)
        return w.reshape(S5_SLABS, MXU_DIM, S5_SLAB_STATE)

    def c_slab(cc):
        w = jnp.einsum('sgcp,gh->sgphc', cc.reshape(slab + (S5_GROUP, S5_STATE)), eye)
        return w.reshape(S5_SLABS, S5_SLAB_STATE, MXU_DIM)

    bw = jnp.concatenate([b_slab(bbar_re), b_slab(bbar_im)], axis=2).astype(BF16)
    cw = jnp.concatenate([c_slab(c_re), -c_slab(c_im)], axis=1).astype(BF16)
    avec = jnp.concatenate([abar_re.reshape(S5_SLABS, S5_SLAB_STATE),
                            abar_im.reshape(S5_SLABS, S5_SLAB_STATE)], axis=1)
    avec = jnp.broadcast_to(avec[:, None, :], (S5_SLABS, BATCH, 2 * S5_SLAB_STATE))
    return bw, avec, cw


def _suffix_matrix(n):
    j = lax.broadcasted_iota(jnp.int32, (n, n), 0)
    s = lax.broadcasted_iota(jnp.int32, (n, n), 1)
    return jnp.where(j > s, -1.0, 0.0).astype(BF16)


def kernel(x, meta_tokens, norm_mix, norm_ffn, s5_a_re, s5_a_im, s5_log_dt, s5_b_re, s5_b_im,
           s5_c_re, s5_c_im, s5_d, s5_w_glu, norm_kv, w_kv, w_q, w_o, w_ffn_in, w_ffn_out,
           norm_final):
    hd = N_HEADS * HEAD_DIM
    row = lambda g: g.reshape(1, D_MODEL)
    bw, avec, cw = _s5_weights(s5_a_re[0], s5_a_im[0], s5_log_dt[0], s5_b_re[0], s5_b_im[0],
                               s5_c_re[0], s5_c_im[0])
    wglu = s5_w_glu[0].astype(BF16)
    wg = [w_ffn_in[i, :, :D_FF].astype(BF16) for i in range(2)]
    wu = [w_ffn_in[i, :, D_FF:].astype(BF16) for i in range(2)]
    wo = [w_ffn_out[i].astype(BF16) for i in range(2)]
    wq = w_q[0].astype(BF16)
    wk = w_kv[:, :hd].astype(BF16)
    wv = w_kv[:, hd:].astype(BF16)
    wo_att = w_o[0].astype(BF16)

    s5 = functools.partial(_s5_layer, gain=row(norm_mix[0]), bw=bw, avec=avec, cw=cw,
                           d_skip=row(s5_d[0]), wglu=wglu)
    x_meta = jnp.broadcast_to(meta_tokens.astype(x.dtype)[None], (BATCH, N_META, D_MODEL))
    zero_state = jnp.zeros((S5_SLABS, BATCH, 2 * S5_SLAB_STATE), F32)
    h_meta, meta_state = s5(x_meta, init_state=zero_state, chunk=N_META)
    h_real, _ = s5(x, init_state=meta_state, chunk=S5_CHUNK)
    h_meta = h_meta[0]
    h_real = h_real.reshape(BATCH * SEQ, D_MODEL)

    ffn0 = functools.partial(_ffn_layer, gain=row(norm_ffn[0]), wg=wg[0], wu=wu[0], wo=wo[0])
    h_meta = ffn0(h_meta, rows_per_step=N_META)
    h_real = ffn0(h_real, rows_per_step=FFN_ROWS)

    proj = functools.partial(_proj_layer, gq=row(norm_mix[1]), gkv=row(norm_kv),
                             wq=wq, wk=wk, wv=wv)
    _, k_meta, v_meta = proj(h_meta, rows_per_step=N_META)
    q, k, v = proj(h_real, rows_per_step=PROJ_ROWS)

    pad = ((0, META_PAD - N_META), (0, 0))
    seq = lambda a: a.reshape(BATCH, SEQ, hd)
    o = _attention(seq(q), seq(k), seq(v), jnp.pad(k_meta, pad), jnp.pad(v_meta, pad))

    out = _out_layer(h_real, o.reshape(BATCH * SEQ, hd), wo_att, row(norm_ffn[1]),
                     wg[1], wu[1], wo[1], row(norm_final), rows_per_step=FFN_ROWS)
    return out.reshape(BATCH, SEQ, D_MODEL)
```

```python
import functools
import math

import jax
import jax.numpy as jnp
from jax import lax
from jax.experimental import pallas as pl
from jax.experimental.pallas import tpu as pltpu

D_MODEL = 1024
BATCH = 8
SEQ = 4096
N_META = 16
S5_GROUP = 16
S5_GROUPS = 64
S5_STATE = 64
N_HEADS = 16
HEAD_DIM = 64
D_FF = 2816
RMS_EPS = 1e-6

LANES = 128
SUBLANES = 8
MXU_DIM = 256
VMEM_LIMIT_BYTES = 56 * 1024 * 1024

S5_SLAB_GROUPS = MXU_DIM // S5_GROUP
S5_SLABS = S5_GROUPS // S5_SLAB_GROUPS
S5_SLAB_STATE = S5_SLAB_GROUPS * S5_STATE
S5_CHUNK = 64

FFN_ROWS = 512
FFN_F_CHUNKS = ((0, 1024), (1024, 2048), (2048, D_FF))
PROJ_ROWS = 512

ATT_Q = 256
ATT_K = 256
ATT_GROUP = 4
ATT_HEADS = 2
ATT_LANES = ATT_HEADS * HEAD_DIM
META_PAD = 128
MASKED_LOGIT = -1e30
REM_FLOOR = -151.0
SOFTPLUS_CUT = 64.0
LOG2E = 1.4426950408889634

F32 = jnp.float32
BF16 = jnp.bfloat16


def _rmsnorm(x, gain):
    ms = jnp.mean(x * x, axis=-1, keepdims=True)
    return x * lax.rsqrt(ms + RMS_EPS) * gain


def _gelu_tanh(y):
    c = math.sqrt(2.0 / math.pi)
    return 0.5 * y * (1.0 + jnp.tanh(c * (y + 0.044715 * (y * y * y))))


def _sigmoid(x):
    return 1.0 / (1.0 + jnp.exp(-x))


def _const_spec(shape):
    nd = len(shape)
    return pl.BlockSpec(shape, lambda *_: (0,) * nd, pipeline_mode=pl.Buffered(1))


def _layer_spec(shape, layer):
    rest = (0,) * (len(shape) - 1)
    return pl.BlockSpec((None,) + tuple(shape[1:]), lambda *_: (layer,) + rest,
                        pipeline_mode=pl.Buffered(1))


def _s5_kernel(x_ref, gain_ref, bw_ref, avec_ref, cw_ref, d_ref, wglu_ref, init_ref,
               out_ref, fin_ref, state_ref, hs_ref, bu_ref, *, chunk):
    half = S5_SLAB_STATE
    lane_slabs = D_MODEL // LANES

    @pl.when(pl.program_id(0) == 0)
    def _():
        state_ref[...] = init_ref[...]

    for b in range(BATCH):
        for j in range(lane_slabs):
            hs_ref[j, pl.ds(b, chunk, stride=BATCH), :] = x_ref[b, :, j * LANES:(j + 1) * LANES]
    h = jnp.concatenate([hs_ref[j] for j in range(lane_slabs)], axis=1)
    u = _rmsnorm(h, gain_ref[...])
    ub = u.astype(BF16)
    zs = []
    for s in range(S5_SLABS):
        chans = slice(s * MXU_DIM, (s + 1) * MXU_DIM)
        bu_ref[...] = jnp.dot(ub[:, chans], bw_ref[s], preferred_element_type=F32)
        a_re = avec_ref[s, :, :half]
        a_im = avec_ref[s, :, half:]

        def step(l, carry, a_re=a_re, a_im=a_im):
            x_re, x_im = carry
            row = pl.multiple_of(l * SUBLANES, SUBLANES)
            b_re = bu_ref[pl.ds(row, SUBLANES), :half]
            b_im = bu_ref[pl.ds(row, SUBLANES), half:]
            n_re = a_re * x_re - a_im * x_im + b_re
            n_im = a_re * x_im + a_im * x_re + b_im
            bu_ref[pl.ds(row, SUBLANES), :half] = n_re
            bu_ref[pl.ds(row, SUBLANES), half:] = n_im
            return n_re, n_im

        x_re, x_im = lax.fori_loop(0, chunk, step,
                                   (state_ref[s, :, :half], state_ref[s, :, half:]), unroll=True)
        state_ref[s, :, :half] = x_re
        state_ref[s, :, half:] = x_im
        y = jnp.dot(bu_ref[...].astype(BF16), cw_ref[s], preferred_element_type=F32)
        zs.append(_gelu_tanh(y + d_ref[:, chans] * u[:, chans]).astype(BF16))

    vg = jnp.dot(jnp.concatenate(zs, axis=1), wglu_ref[...], preferred_element_type=F32)
    res = h + vg[:, :D_MODEL] * _sigmoid(vg[:, D_MODEL:])
    for j in range(lane_slabs):
        hs_ref[j] = res[:, j * LANES:(j + 1) * LANES]
    for b in range(BATCH):
        for j in range(lane_slabs):
            out_ref[b, :, j * LANES:(j + 1) * LANES] = hs_ref[j, pl.ds(b, chunk, stride=BATCH), :]

    @pl.when(pl.program_id(0) == pl.num_programs(0) - 1)
    def _():
        fin_ref[...] = state_ref[...]


def _s5_layer(x, gain, bw, avec, cw, d_skip, wglu, init_state, *, chunk):
    length = x.shape[1]
    m = chunk * BATCH
    assert length % chunk == 0
    state_shape = (S5_SLABS, BATCH, 2 * S5_SLAB_STATE)
    io_spec = pl.BlockSpec((BATCH, chunk, D_MODEL), lambda i: (0, i, 0))
    return pl.pallas_call(
        functools.partial(_s5_kernel, chunk=chunk),
        grid=(length // chunk,),
        in_specs=[
            io_spec,
            _const_spec((1, D_MODEL)),
            _const_spec(bw.shape),
            _const_spec(avec.shape),
            _const_spec(cw.shape),
            _const_spec((1, D_MODEL)),
            _const_spec(wglu.shape),
            _const_spec(state_shape),
        ],
        out_specs=[
            io_spec,
            pl.BlockSpec(state_shape, lambda i: (0, 0, 0)),
        ],
        out_shape=[
            jax.ShapeDtypeStruct(x.shape, F32),
            jax.ShapeDtypeStruct(state_shape, F32),
        ],
        scratch_shapes=[
            pltpu.VMEM(state_shape, F32),
            pltpu.VMEM((D_MODEL // LANES, m, LANES), F32),
            pltpu.VMEM((m, 2 * S5_SLAB_STATE), F32),
        ],
        compiler_params=pltpu.CompilerParams(
            dimension_semantics=("arbitrary",), vmem_limit_bytes=VMEM_LIMIT_BYTES),
        name="s5_layer",
    )(x, gain, bw, avec, cw, d_skip, wglu, init_state)


def _swiglu(hn, win_ref, wo_ref):
    acc = None
    for f0, f1 in FFN_F_CHUNKS:
        g = jnp.dot(hn, win_ref[:, f0:f1], preferred_element_type=F32)
        u = jnp.dot(hn, win_ref[:, D_FF + f0:D_FF + f1], preferred_element_type=F32)
        a = (g * _sigmoid(g) * u).astype(BF16)
        part = jnp.dot(a, wo_ref[f0:f1, :], preferred_element_type=F32)
        acc = part if acc is None else acc + part
    return acc


def _ffn_kernel(h_ref, gain_ref, win_ref, wo_ref, out_ref):
    h = h_ref[...]
    hn = _rmsnorm(h, gain_ref[...]).astype(BF16)
    out_ref[...] = h + _swiglu(hn, win_ref, wo_ref)


def _ffn_layer(h, gain, w_in, w_out, *, layer, rows_per_step):
    rows = h.shape[0]
    assert rows % rows_per_step == 0
    return pl.pallas_call(
        _ffn_kernel,
        grid=(rows // rows_per_step,),
        in_specs=[
            pl.BlockSpec((rows_per_step, D_MODEL), lambda i: (i, 0)),
            _const_spec((1, D_MODEL)),
            _layer_spec(w_in.shape, layer),
            _layer_spec(w_out.shape, layer),
        ],
        out_specs=pl.BlockSpec((rows_per_step, D_MODEL), lambda i: (i, 0)),
        out_shape=jax.ShapeDtypeStruct((rows, D_MODEL), F32),
        compiler_params=pltpu.CompilerParams(
            dimension_semantics=("parallel",), vmem_limit_bytes=VMEM_LIMIT_BYTES),
        name="ffn_layer",
    )(h, gain, w_in, w_out)


def _proj_kernel(h_ref, gq_ref, gkv_ref, wq_ref, wkv_ref, q_ref, k_ref, v_ref):
    h = h_ref[...]
    y = h * lax.rsqrt(jnp.mean(h * h, axis=-1, keepdims=True) + RMS_EPS)
    hq = (y * gq_ref[...]).astype(BF16)
    hkv = (y * gkv_ref[...]).astype(BF16)
    scale = LOG2E / math.sqrt(HEAD_DIM)
    q_ref[...] = (jnp.dot(hq, wq_ref[...], preferred_element_type=F32) * scale).astype(BF16)
    hd = q_ref.shape[1]
    k_ref[...] = jnp.dot(hkv, wkv_ref[:, :hd], preferred_element_type=F32).astype(BF16)
    v_ref[...] = jnp.dot(hkv, wkv_ref[:, hd:], preferred_element_type=F32).astype(BF16)


def _proj_layer(h, gq, gkv, wq, wkv, *, rows_per_step):
    rows = h.shape[0]
    assert rows % rows_per_step == 0
    hd = N_HEADS * HEAD_DIM
    row_spec = pl.BlockSpec((rows_per_step, hd), lambda i: (i, 0))
    return pl.pallas_call(
        _proj_kernel,
        grid=(rows // rows_per_step,),
        in_specs=[
            pl.BlockSpec((rows_per_step, D_MODEL), lambda i: (i, 0)),
            _const_spec((1, D_MODEL)),
            _const_spec((1, D_MODEL)),
            _const_spec(wq.shape),
            _const_spec(wkv.shape),
        ],
        out_specs=[row_spec, row_spec, row_spec],
        out_shape=[jax.ShapeDtypeStruct((rows, hd), BF16)] * 3,
        compiler_params=pltpu.CompilerParams(
            dimension_semantics=("parallel",), vmem_limit_bytes=VMEM_LIMIT_BYTES),
        name="qkv_proj",
    )(h, gq, gkv, wq, wkv)


def _attn_kernel(q_ref, k_ref, v_ref, km_ref, vm_ref, tri_ref, trim_ref, dbias_ref, mbias_ref,
                 o_ref, acc_ref, rem_ref):
    head_of_lane = lax.broadcasted_iota(jnp.int32, (1, ATT_LANES), 1) // HEAD_DIM

    def block(t, q_st, kb, vb, tri_neg, bias):
        nk = kb.shape[0]
        z = lax.dot_general(q_st, kb, (((1,), (1,)), ((), ())),
                            preferred_element_type=F32)
        if bias is not None:
            z = z + bias
        sp = jnp.where(z > SOFTPLUS_CUT, z, jnp.log2(1.0 + jnp.exp2(z)))
        suffix = jnp.dot(sp.astype(BF16), tri_neg, preferred_element_type=F32)
        rem = rem_ref[t]
        rem_b = jnp.concatenate([rem] * (nk // LANES), axis=1)
        w = jnp.exp2((z - sp) + suffix + rem_b)
        acc_ref[t] += jnp.dot(w.astype(BF16), vb, preferred_element_type=F32)
        rem_ref[t] = rem - jnp.sum(sp, axis=1, keepdims=True)

    def kv_block(j):
        k0 = pl.multiple_of(j * ATT_K, ATT_K)
        return k_ref[pl.ds(k0, ATT_K), :], v_ref[pl.ds(k0, ATT_K), :]

    def stacked_q(qi):
        q = q_ref[pl.ds(pl.multiple_of(qi * ATT_Q, ATT_Q), ATT_Q), :]
        zero = jnp.zeros_like(q)
        return jnp.concatenate([jnp.where(head_of_lane == hd, q, zero) for hd in range(ATT_HEADS)],
                               axis=0)

    def live_flag(t):
        return (jnp.max(rem_ref[t]) >= REM_FLOOR).astype(jnp.int32)

    def finish(t, qi, q_st, live0, j0):
        def earlier(c):
            j, _ = c
            block(t, q_st, *kv_block(j), tri_ref[...], None)
            return j - 1, live_flag(t)

        _, live = lax.while_loop(lambda c: jnp.logical_and(c[0] >= 0, c[1] > 0), earlier,
                                 (j0, live0))

        @pl.when(live > 0)
        def _():
            block(t, q_st, km_ref[...], vm_ref[...], trim_ref[...], mbias_ref[...])

        o_t = acc_ref[t, :ATT_Q]
        for hd in range(1, ATT_HEADS):
            o_t = jnp.where(head_of_lane == hd, acc_ref[t, hd * ATT_Q:(hd + 1) * ATT_Q], o_t)
        o_ref[pl.ds(pl.multiple_of(qi * ATT_Q, ATT_Q), ATT_Q), :] = o_t.astype(o_ref.dtype)

    def tile_group(q_first, first):
        tiles = [q_first + t for t in range(ATT_GROUP)]
        qs = [stacked_q(qi) for qi in tiles]
        kvs = [kv_block(qi) for qi in tiles]
        acc_ref[...] = jnp.zeros_like(acc_ref)
        rem_ref[...] = jnp.zeros_like(rem_ref)
        for t in range(ATT_GROUP):
            block(t, qs[t], *kvs[t], tri_ref[...], dbias_ref[...])
            if t > 0:
                block(t, qs[t], *kvs[t - 1], tri_ref[...], None)
            elif not first:
                block(t, qs[t], *kv_block(q_first - 1), tri_ref[...], None)
        lives = [jnp.int32(1) if (first and t == 0) else live_flag(t) for t in range(ATT_GROUP)]
        for t in range(ATT_GROUP):
            finish(t, tiles[t], qs[t], lives[t], tiles[t] - 2)

    tile_group(0, True)

    def later_group(i, carry):
        tile_group(ATT_GROUP * i, False)
        return carry

    lax.fori_loop(1, SEQ // (ATT_GROUP * ATT_Q), later_group, 0)


def _attention(q, k, v, k_meta, v_meta):
    pairs = N_HEADS // ATT_HEADS
    seq_spec = pl.BlockSpec((None, SEQ, ATT_LANES), lambda b, p: (b, 0, p))
    meta_spec = pl.BlockSpec((META_PAD, ATT_LANES), lambda b, p: (0, p))
    tri = _suffix_matrix(ATT_K)
    tri_meta = _suffix_matrix(META_PAD)
    row = lax.broadcasted_iota(jnp.int32, (ATT_Q, ATT_K), 0)
    col = lax.broadcasted_iota(jnp.int32, (ATT_Q, ATT_K), 1)
    diag_bias = jnp.where(col < row, 0.0, MASKED_LOGIT).astype(F32)
    diag_bias = jnp.concatenate([diag_bias] * ATT_HEADS, axis=0)
    mcol = lax.broadcasted_iota(jnp.int32, (1, META_PAD), 1)
    meta_bias = jnp.where(mcol < N_META, 0.0, MASKED_LOGIT).astype(F32)
    return pl.pallas_call(
        _attn_kernel,
        grid=(BATCH, pairs),
        in_specs=[
            seq_spec, seq_spec, seq_spec, meta_spec, meta_spec,
            _const_spec(tri.shape), _const_spec(tri_meta.shape),
            _const_spec(diag_bias.shape), _const_spec(meta_bias.shape),
        ],
        out_specs=seq_spec,
        out_shape=jax.ShapeDtypeStruct(q.shape, BF16),
        scratch_shapes=[
            pltpu.VMEM((ATT_GROUP, ATT_HEADS * ATT_Q, ATT_LANES), F32),
            pltpu.VMEM((ATT_GROUP, ATT_HEADS * ATT_Q, LANES), F32),
        ],
        compiler_params=pltpu.CompilerParams(
            dimension_semantics=("parallel", "parallel"), vmem_limit_bytes=VMEM_LIMIT_BYTES),
        name="stickbreak_attn",
    )(q, k, v, k_meta, v_meta, tri, tri_meta, diag_bias, meta_bias)


def _out_kernel(h_ref, o_ref, wo_att_ref, gain_ref, win_ref, wo_ref, gfin_ref, out_ref):
    h = h_ref[...] + jnp.dot(o_ref[...], wo_att_ref[...], preferred_element_type=F32)
    hn = _rmsnorm(h, gain_ref[...]).astype(BF16)
    h = h + _swiglu(hn, win_ref, wo_ref)
    out_ref[...] = _rmsnorm(h, gfin_ref[...])


def _out_layer(h, o, wo_att, gain, w_in, w_out, gfin, *, layer, rows_per_step):
    rows = h.shape[0]
    assert rows % rows_per_step == 0
    row_spec = pl.BlockSpec((rows_per_step, D_MODEL), lambda i: (i, 0))
    return pl.pallas_call(
        _out_kernel,
        grid=(rows // rows_per_step,),
        in_specs=[
            row_spec,
            pl.BlockSpec((rows_per_step, o.shape[1]), lambda i: (i, 0)),
            _const_spec(wo_att.shape),
            _const_spec((1, D_MODEL)),
            _layer_spec(w_in.shape, layer),
            _layer_spec(w_out.shape, layer),
            _const_spec((1, D_MODEL)),
        ],
        out_specs=row_spec,
        out_shape=jax.ShapeDtypeStruct((rows, D_MODEL), F32),
        compiler_params=pltpu.CompilerParams(
            dimension_semantics=("parallel",), vmem_limit_bytes=VMEM_LIMIT_BYTES),
        name="oproj_ffn_final",
    )(h, o, wo_att, gain, w_in, w_out, gfin)


def _s5_weights(a_re, a_im, log_dt, b_re, b_im, c_re, c_im):
    dt = jnp.exp(log_dt)[:, None]
    mag = jnp.exp(dt * a_re)
    ang = dt * a_im
    abar_re = mag * jnp.cos(ang)
    abar_im = mag * jnp.sin(ang)
    den = a_re * a_re + a_im * a_im
    coef_re = ((abar_re - 1.0) * a_re + abar_im * a_im) / den
    coef_im = (abar_im * a_re - (abar_re - 1.0) * a_im) / den
    bbar_re = coef_re[..., None] * b_re - coef_im[..., None] * b_im
    bbar_im = coef_re[..., None] * b_im + coef_im[..., None] * b_re
    eye = jnp.eye(S5_SLAB_GROUPS, dtype=F32)
    slab = (S5_SLABS, S5_SLAB_GROUPS)

    def b_slab(bb):
        w = jnp.einsum('sgpc,gh->sgchp', bb.reshape(slab + (S5_STATE, S5_GROUP)), eye)
        return w.reshape(S5_SLABS, MXU_DIM, S5_SLAB_STATE)

    def c_slab(cc):
        w = jnp.einsum('sgcp,gh->sgphc', cc.reshape(slab + (S5_GROUP, S5_STATE)), eye)
        return w.reshape(S5_SLABS, S5_SLAB_STATE, MXU_DIM)

    bw = jnp.concatenate([b_slab(bbar_re), b_slab(bbar_im)], axis=2).astype(BF16)
    cw = jnp.concatenate([c_slab(c_re), -c_slab(c_im)], axis=1).astype(BF16)
    avec = jnp.concatenate([abar_re.reshape(S5_SLABS, S5_SLAB_STATE),
                            abar_im.reshape(S5_SLABS, S5_SLAB_STATE)], axis=1)
    avec = jnp.broadcast_to(avec[:, None, :], (S5_SLABS, BATCH, 2 * S5_SLAB_STATE))
    return bw, avec, cw


def _suffix_matrix(n):
    j = lax.broadcasted_iota(jnp.int32, (n, n), 0)
    s = lax.broadcasted_iota(jnp.int32, (n, n), 1)
    return jnp.where(j > s, -1.0, 0.0).astype(BF16)


def kernel(x, meta_tokens, norm_mix, norm_ffn, s5_a_re, s5_a_im, s5_log_dt, s5_b_re, s5_b_im,
           s5_c_re, s5_c_im, s5_d, s5_w_glu, norm_kv, w_kv, w_q, w_o, w_ffn_in, w_ffn_out,
           norm_final):
    hd = N_HEADS * HEAD_DIM
    row = lambda g: g.reshape(1, D_MODEL)
    bw, avec, cw = _s5_weights(s5_a_re[0], s5_a_im[0], s5_log_dt[0], s5_b_re[0], s5_b_im[0],
                               s5_c_re[0], s5_c_im[0])
    wglu = s5_w_glu[0].astype(BF16)
    w_in = w_ffn_in.astype(BF16)
    w_out = w_ffn_out.astype(BF16)
    wq = w_q[0].astype(BF16)
    wkv = w_kv.astype(BF16)
    wo_att = w_o[0].astype(BF16)

    s5 = functools.partial(_s5_layer, gain=row(norm_mix[0]), bw=bw, avec=avec, cw=cw,
                           d_skip=row(s5_d[0]), wglu=wglu)
    x_meta = jnp.broadcast_to(meta_tokens.astype(x.dtype)[None], (BATCH, N_META, D_MODEL))
    zero_state = jnp.zeros((S5_SLABS, BATCH, 2 * S5_SLAB_STATE), F32)
    h_meta, meta_state = s5(x_meta, init_state=zero_state, chunk=N_META)
    h_real, _ = s5(x, init_state=meta_state, chunk=S5_CHUNK)
    h_meta = h_meta[0]
    h_real = h_real.reshape(BATCH * SEQ, D_MODEL)

    ffn0 = functools.partial(_ffn_layer, gain=row(norm_ffn[0]), w_in=w_in, w_out=w_out, layer=0)
    h_meta = ffn0(h_meta, rows_per_step=N_META)
    h_real = ffn0(h_real, rows_per_step=FFN_ROWS)

    proj = functools.partial(_proj_layer, gq=row(norm_mix[1]), gkv=row(norm_kv),
                             wq=wq, wkv=wkv)
    _, k_meta, v_meta = proj(h_meta, rows_per_step=N_META)
    q, k, v = proj(h_real, rows_per_step=PROJ_ROWS)

    pad = ((0, META_PAD - N_META), (0, 0))
    seq = lambda a: a.reshape(BATCH, SEQ, hd)
    o = _attention(seq(q), seq(k), seq(v), jnp.pad(k_meta, pad), jnp.pad(v_meta, pad))

    out = _out_layer(h_real, o.reshape(BATCH * SEQ, hd), wo_att, row(norm_ffn[1]),
                     w_in, w_out, row(norm_final), layer=1, rows_per_step=FFN_ROWS)
    return out.reshape(BATCH, SEQ, D_MODEL)
```

```python
import functools
import math

import jax
import jax.numpy as jnp
from jax import lax
from jax.experimental import pallas as pl
from jax.experimental.pallas import tpu as pltpu

D_MODEL = 1024
BATCH = 8
SEQ = 4096
N_META = 16
S5_GROUP = 16
S5_GROUPS = 64
S5_STATE = 64
N_HEADS = 16
HEAD_DIM = 64
D_FF = 2816
RMS_EPS = 1e-6

LANES = 128
SUBLANES = 8
MXU_DIM = 256
VMEM_LIMIT_BYTES = 56 * 1024 * 1024

S5_SLAB_GROUPS = MXU_DIM // S5_GROUP
S5_SLABS = S5_GROUPS // S5_SLAB_GROUPS
S5_SLAB_STATE = S5_SLAB_GROUPS * S5_STATE
S5_CHUNK = 64

FFN_ROWS = 1024
FFN_F_CHUNKS = ((0, 1024), (1024, 2048), (2048, D_FF))
PROJ_ROWS = 1024

ATT_Q = 256
ATT_K = 256
ATT_GROUP = 4
ATT_HEADS = 2
ATT_LANES = ATT_HEADS * HEAD_DIM
META_PAD = 128
MASKED_LOGIT = -1e30
REM_FLOOR = -151.0
SOFTPLUS_CUT = 64.0
LOG2E = 1.4426950408889634

F32 = jnp.float32
BF16 = jnp.bfloat16


def _rmsnorm(x, gain):
    ms = jnp.mean(x * x, axis=-1, keepdims=True)
    return x * lax.rsqrt(ms + RMS_EPS) * gain


def _gelu_tanh(y):
    c = math.sqrt(2.0 / math.pi)
    return 0.5 * y * (1.0 + jnp.tanh(c * (y + 0.044715 * (y * y * y))))


def _sigmoid(x):
    return 1.0 / (1.0 + jnp.exp(-x))


def _const_spec(shape):
    nd = len(shape)
    return pl.BlockSpec(shape, lambda *_: (0,) * nd, pipeline_mode=pl.Buffered(1))


def _layer_spec(shape, layer):
    rest = (0,) * (len(shape) - 1)
    return pl.BlockSpec((None,) + tuple(shape[1:]), lambda *_: (layer,) + rest,
                        pipeline_mode=pl.Buffered(1))


def _s5_kernel(x_ref, gain_ref, bw_ref, avec_ref, cw_ref, d_ref, wglu_ref, init_ref,
               out_ref, fin_ref, state_ref, hs_ref, bu_ref, *, chunk):
    half = S5_SLAB_STATE
    lane_slabs = D_MODEL // LANES

    @pl.when(pl.program_id(0) == 0)
    def _():
        state_ref[...] = init_ref[...]

    for b in range(BATCH):
        for j in range(lane_slabs):
            hs_ref[j, pl.ds(b, chunk, stride=BATCH), :] = x_ref[b, :, j * LANES:(j + 1) * LANES]
    h = jnp.concatenate([hs_ref[j] for j in range(lane_slabs)], axis=1)
    u = _rmsnorm(h, gain_ref[...])
    ub = u.astype(BF16)
    zs = []
    for s in range(S5_SLABS):
        chans = slice(s * MXU_DIM, (s + 1) * MXU_DIM)
        bu_ref[...] = jnp.dot(ub[:, chans], bw_ref[s], preferred_element_type=F32)
        a_re = avec_ref[s, :, :half]
        a_im = avec_ref[s, :, half:]

        def step(l, carry, a_re=a_re, a_im=a_im):
            x_re, x_im = carry
            row = pl.multiple_of(l * SUBLANES, SUBLANES)
            b_re = bu_ref[pl.ds(row, SUBLANES), :half]
            b_im = bu_ref[pl.ds(row, SUBLANES), half:]
            n_re = a_re * x_re - a_im * x_im + b_re
            n_im = a_re * x_im + a_im * x_re + b_im
            bu_ref[pl.ds(row, SUBLANES), :half] = n_re
            bu_ref[pl.ds(row, SUBLANES), half:] = n_im
            return n_re, n_im

        x_re, x_im = lax.fori_loop(0, chunk, step,
                                   (state_ref[s, :, :half], state_ref[s, :, half:]), unroll=True)
        state_ref[s, :, :half] = x_re
        state_ref[s, :, half:] = x_im
        y = jnp.dot(bu_ref[...].astype(BF16), cw_ref[s], preferred_element_type=F32)
        zs.append(_gelu_tanh(y + d_ref[:, chans] * u[:, chans]).astype(BF16))

    vg = jnp.dot(jnp.concatenate(zs, axis=1), wglu_ref[...], preferred_element_type=F32)
    res = h + vg[:, :D_MODEL] * _sigmoid(vg[:, D_MODEL:])
    for j in range(lane_slabs):
        hs_ref[j] = res[:, j * LANES:(j + 1) * LANES]
    for b in range(BATCH):
        for j in range(lane_slabs):
            out_ref[b, :, j * LANES:(j + 1) * LANES] = hs_ref[j, pl.ds(b, chunk, stride=BATCH), :]

    @pl.when(pl.program_id(0) == pl.num_programs(0) - 1)
    def _():
        fin_ref[...] = state_ref[...]


def _s5_layer(x, gain, bw, avec, cw, d_skip, wglu, init_state, *, chunk):
    length = x.shape[1]
    m = chunk * BATCH
    assert length % chunk == 0
    state_shape = (S5_SLABS, BATCH, 2 * S5_SLAB_STATE)
    io_spec = pl.BlockSpec((BATCH, chunk, D_MODEL), lambda i: (0, i, 0))
    return pl.pallas_call(
        functools.partial(_s5_kernel, chunk=chunk),
        grid=(length // chunk,),
        in_specs=[
            io_spec,
            _const_spec((1, D_MODEL)),
            _const_spec(bw.shape),
            _const_spec(avec.shape),
            _const_spec(cw.shape),
            _const_spec((1, D_MODEL)),
            _const_spec(wglu.shape),
            _const_spec(state_shape),
        ],
        out_specs=[
            io_spec,
            pl.BlockSpec(state_shape, lambda i: (0, 0, 0)),
        ],
        out_shape=[
            jax.ShapeDtypeStruct(x.shape, F32),
            jax.ShapeDtypeStruct(state_shape, F32),
        ],
        scratch_shapes=[
            pltpu.VMEM(state_shape, F32),
            pltpu.VMEM((D_MODEL // LANES, m, LANES), F32),
            pltpu.VMEM((m, 2 * S5_SLAB_STATE), F32),
        ],
        compiler_params=pltpu.CompilerParams(
            dimension_semantics=("arbitrary",), vmem_limit_bytes=VMEM_LIMIT_BYTES),
        name="s5_layer",
    )(x, gain, bw, avec, cw, d_skip, wglu, init_state)


def _swiglu(hn, win_ref, wo_ref):
    acc = None
    for f0, f1 in FFN_F_CHUNKS:
        g = jnp.dot(hn, win_ref[:, f0:f1], preferred_element_type=F32)
        u = jnp.dot(hn, win_ref[:, D_FF + f0:D_FF + f1], preferred_element_type=F32)
        a = (g * _sigmoid(g) * u).astype(BF16)
        part = jnp.dot(a, wo_ref[f0:f1, :], preferred_element_type=F32)
        acc = part if acc is None else acc + part
    return acc


def _ffn_kernel(h_ref, gain_ref, win_ref, wo_ref, out_ref):
    h = h_ref[...]
    hn = _rmsnorm(h, gain_ref[...]).astype(BF16)
    out_ref[...] = h + _swiglu(hn, win_ref, wo_ref)


def _ffn_layer(h, gain, w_in, w_out, *, layer, rows_per_step):
    rows = h.shape[0]
    assert rows % rows_per_step == 0
    return pl.pallas_call(
        _ffn_kernel,
        grid=(rows // rows_per_step,),
        in_specs=[
            pl.BlockSpec((rows_per_step, D_MODEL), lambda i: (i, 0)),
            _const_spec((1, D_MODEL)),
            _layer_spec(w_in.shape, layer),
            _layer_spec(w_out.shape, layer),
        ],
        out_specs=pl.BlockSpec((rows_per_step, D_MODEL), lambda i: (i, 0)),
        out_shape=jax.ShapeDtypeStruct((rows, D_MODEL), F32),
        compiler_params=pltpu.CompilerParams(
            dimension_semantics=("parallel",), vmem_limit_bytes=VMEM_LIMIT_BYTES),
        name="ffn_layer",
    )(h, gain, w_in, w_out)


def _proj_kernel(h_ref, gq_ref, gkv_ref, wq_ref, wkv_ref, q_ref, k_ref, v_ref):
    h = h_ref[...]
    y = h * lax.rsqrt(jnp.mean(h * h, axis=-1, keepdims=True) + RMS_EPS)
    hq = (y * gq_ref[...]).astype(BF16)
    hkv = (y * gkv_ref[...]).astype(BF16)
    scale = LOG2E / math.sqrt(HEAD_DIM)
    q_ref[...] = (jnp.dot(hq, wq_ref[...], preferred_element_type=F32) * scale).astype(BF16)
    hd = q_ref.shape[1]
    k_ref[...] = jnp.dot(hkv, wkv_ref[:, :hd], preferred_element_type=F32).astype(BF16)
    v_ref[...] = jnp.dot(hkv, wkv_ref[:, hd:], preferred_element_type=F32).astype(BF16)


def _proj_layer(h, gq, gkv, wq, wkv, *, rows_per_step):
    rows = h.shape[0]
    assert rows % rows_per_step == 0
    hd = N_HEADS * HEAD_DIM
    row_spec = pl.BlockSpec((rows_per_step, hd), lambda i: (i, 0))
    return pl.pallas_call(
        _proj_kernel,
        grid=(rows // rows_per_step,),
        in_specs=[
            pl.BlockSpec((rows_per_step, D_MODEL), lambda i: (i, 0)),
            _const_spec((1, D_MODEL)),
            _const_spec((1, D_MODEL)),
            _const_spec(wq.shape),
            _const_spec(wkv.shape),
        ],
        out_specs=[row_spec, row_spec, row_spec],
        out_shape=[jax.ShapeDtypeStruct((rows, hd), BF16)] * 3,
        compiler_params=pltpu.CompilerParams(
            dimension_semantics=("parallel",), vmem_limit_bytes=VMEM_LIMIT_BYTES),
        name="qkv_proj",
    )(h, gq, gkv, wq, wkv)


def _attn_kernel(q_ref, k_ref, v_ref, km_ref, vm_ref, tri_ref, trim_ref, dbias_ref, mbias_ref,
                 o_ref, acc_ref, rem_ref):
    head_of_lane = lax.broadcasted_iota(jnp.int32, (1, ATT_LANES), 1) // HEAD_DIM

    def block(t, q_st, kb, vb, tri_neg, bias):
        nk = kb.shape[0]
        z = lax.dot_general(q_st, kb, (((1,), (1,)), ((), ())),
                            preferred_element_type=F32)
        if bias is not None:
            z = z + bias
        sp = jnp.where(z > SOFTPLUS_CUT, z, jnp.log2(1.0 + jnp.exp2(z)))
        suffix = jnp.dot(sp.astype(BF16), tri_neg, preferred_element_type=F32)
        rem = rem_ref[t]
        rem_b = jnp.concatenate([rem] * (nk // LANES), axis=1)
        w = jnp.exp2((z - sp) + suffix + rem_b)
        acc_ref[t] += jnp.dot(w.astype(BF16), vb, preferred_element_type=F32)
        rem_ref[t] = rem - jnp.sum(sp, axis=1, keepdims=True)

    def kv_block(j):
        k0 = pl.multiple_of(j * ATT_K, ATT_K)
        return k_ref[pl.ds(k0, ATT_K), :], v_ref[pl.ds(k0, ATT_K), :]

    def stacked_q(qi):
        q = q_ref[pl.ds(pl.multiple_of(qi * ATT_Q, ATT_Q), ATT_Q), :]
        zero = jnp.zeros_like(q)
        return jnp.concatenate([jnp.where(head_of_lane == hd, q, zero) for hd in range(ATT_HEADS)],
                               axis=0)

    def live_flag(t):
        return (jnp.max(rem_ref[t]) >= REM_FLOOR).astype(jnp.int32)

    def finish(t, qi, q_st, live0, j0):
        def earlier(c):
            j, _ = c
            block(t, q_st, *kv_block(j), tri_ref[...], None)
            return j - 1, live_flag(t)

        _, live = lax.while_loop(lambda c: jnp.logical_and(c[0] >= 0, c[1] > 0), earlier,
                                 (j0, live0))

        @pl.when(live > 0)
        def _():
            block(t, q_st, km_ref[...], vm_ref[...], trim_ref[...], mbias_ref[...])

        o_t = acc_ref[t, :ATT_Q]
        for hd in range(1, ATT_HEADS):
            o_t = jnp.where(head_of_lane == hd, acc_ref[t, hd * ATT_Q:(hd + 1) * ATT_Q], o_t)
        o_ref[pl.ds(pl.multiple_of(qi * ATT_Q, ATT_Q), ATT_Q), :] = o_t.astype(o_ref.dtype)

    def tile_group(q_first, first):
        tiles = [q_first + t for t in range(ATT_GROUP)]
        qs = [stacked_q(qi) for qi in tiles]
        kvs = [kv_block(qi) for qi in tiles]
        acc_ref[...] = jnp.zeros_like(acc_ref)
        rem_ref[...] = jnp.zeros_like(rem_ref)
        for t in range(ATT_GROUP):
            block(t, qs[t], *kvs[t], tri_ref[...], dbias_ref[...])
            if t > 0:
                block(t, qs[t], *kvs[t - 1], tri_ref[...], None)
            elif not first:
                block(t, qs[t], *kv_block(q_first - 1), tri_ref[...], None)
        lives = [jnp.int32(1) if (first and t == 0) else live_flag(t) for t in range(ATT_GROUP)]
        for t in range(ATT_GROUP):
            finish(t, tiles[t], qs[t], lives[t], tiles[t] - 2)

    tile_group(0, True)

    def later_group(i, carry):
        tile_group(ATT_GROUP * i, False)
        return carry

    lax.fori_loop(1, SEQ // (ATT_GROUP * ATT_Q), later_group, 0)


def _attention(q, k, v, k_meta, v_meta):
    pairs = N_HEADS // ATT_HEADS
    seq_spec = pl.BlockSpec((None, SEQ, ATT_LANES), lambda b, p: (b, 0, p))
    meta_spec = pl.BlockSpec((META_PAD, ATT_LANES), lambda b, p: (0, p))
    tri = _suffix_matrix(ATT_K)
    tri_meta = _suffix_matrix(META_PAD)
    row = lax.broadcasted_iota(jnp.int32, (ATT_Q, ATT_K), 0)
    col = lax.broadcasted_iota(jnp.int32, (ATT_Q, ATT_K), 1)
    diag_bias = jnp.where(col < row, 0.0, MASKED_LOGIT).astype(F32)
    diag_bias = jnp.concatenate([diag_bias] * ATT_HEADS, axis=0)
    mcol = lax.broadcasted_iota(jnp.int32, (1, META_PAD), 1)
    meta_bias = jnp.where(mcol < N_META, 0.0, MASKED_LOGIT).astype(F32)
    return pl.pallas_call(
        _attn_kernel,
        grid=(BATCH, pairs),
        in_specs=[
            seq_spec, seq_spec, seq_spec, meta_spec, meta_spec,
            _const_spec(tri.shape), _const_spec(tri_meta.shape),
            _const_spec(diag_bias.shape), _const_spec(meta_bias.shape),
        ],
        out_specs=seq_spec,
        out_shape=jax.ShapeDtypeStruct(q.shape, BF16),
        scratch_shapes=[
            pltpu.VMEM((ATT_GROUP, ATT_HEADS * ATT_Q, ATT_LANES), F32),
            pltpu.VMEM((ATT_GROUP, ATT_HEADS * ATT_Q, LANES), F32),
        ],
        compiler_params=pltpu.CompilerParams(
            dimension_semantics=("parallel", "parallel"), vmem_limit_bytes=VMEM_LIMIT_BYTES),
        name="stickbreak_attn",
    )(q, k, v, k_meta, v_meta, tri, tri_meta, diag_bias, meta_bias)


def _out_kernel(h_ref, o_ref, wo_att_ref, gain_ref, win_ref, wo_ref, gfin_ref, out_ref):
    h = h_ref[...] + jnp.dot(o_ref[...], wo_att_ref[...], preferred_element_type=F32)
    hn = _rmsnorm(h, gain_ref[...]).astype(BF16)
    h = h + _swiglu(hn, win_ref, wo_ref)
    out_ref[...] = _rmsnorm(h, gfin_ref[...])


def _out_layer(h, o, wo_att, gain, w_in, w_out, gfin, *, layer, rows_per_step):
    rows = h.shape[0]
    assert rows % rows_per_step == 0
    row_spec = pl.BlockSpec((rows_per_step, D_MODEL), lambda i: (i, 0))
    return pl.pallas_call(
        _out_kernel,
        grid=(rows // rows_per_step,),
        in_specs=[
            row_spec,
            pl.BlockSpec((rows_per_step, o.shape[1]), lambda i: (i, 0)),
            _const_spec(wo_att.shape),
            _const_spec((1, D_MODEL)),
            _layer_spec(w_in.shape, layer),
            _layer_spec(w_out.shape, layer),
            _const_spec((1, D_MODEL)),
        ],
        out_specs=row_spec,
        out_shape=jax.ShapeDtypeStruct((rows, D_MODEL), F32),
        compiler_params=pltpu.CompilerParams(
            dimension_semantics=("parallel",), vmem_limit_bytes=VMEM_LIMIT_BYTES),
        name="oproj_ffn_final",
    )(h, o, wo_att, gain, w_in, w_out, gfin)


def _s5_weights(a_re, a_im, log_dt, b_re, b_im, c_re, c_im):
    dt = jnp.exp(log_dt)[:, None]
    mag = jnp.exp(dt * a_re)
    ang = dt * a_im
    abar_re = mag * jnp.cos(ang)
    abar_im = mag * jnp.sin(ang)
    den = a_re * a_re + a_im * a_im
    coef_re = ((abar_re - 1.0) * a_re + abar_im * a_im) / den
    coef_im = (abar_im * a_re - (abar_re - 1.0) * a_im) / den
    bbar_re = coef_re[..., None] * b_re - coef_im[..., None] * b_im
    bbar_im = coef_re[..., None] * b_im + coef_im[..., None] * b_re
    same_group = jnp.eye(S5_SLAB_GROUPS, dtype=bool)[None, :, None, :, None]
    slab = (S5_SLABS, S5_SLAB_GROUPS)

    def block_diag(w):
        full = jnp.where(same_group, w[:, :, :, None, :], 0.0)
        return full.reshape(S5_SLABS, S5_SLAB_GROUPS * w.shape[2], S5_SLAB_GROUPS * w.shape[3])

    def b_slab(bb):
        return block_diag(jnp.swapaxes(bb.reshape(slab + (S5_STATE, S5_GROUP)), 2, 3))

    def c_slab(cc):
        return block_diag(jnp.swapaxes(cc.reshape(slab + (S5_GROUP, S5_STATE)), 2, 3))

    bw = jnp.concatenate([b_slab(bbar_re), b_slab(bbar_im)], axis=2).astype(BF16)
    cw = jnp.concatenate([c_slab(c_re), -c_slab(c_im)], axis=1).astype(BF16)
    avec = jnp.concatenate([abar_re.reshape(S5_SLABS, S5_SLAB_STATE),
                            abar_im.reshape(S5_SLABS, S5_SLAB_STATE)], axis=1)
    avec = jnp.broadcast_to(avec[:, None, :], (S5_SLABS, BATCH, 2 * S5_SLAB_STATE))
    return bw, avec, cw


def _suffix_matrix(n):
    j = lax.broadcasted_iota(jnp.int32, (n, n), 0)
    s = lax.broadcasted_iota(jnp.int32, (n, n), 1)
    return jnp.where(j > s, -1.0, 0.0).astype(BF16)


def kernel(x, meta_tokens, norm_mix, norm_ffn, s5_a_re, s5_a_im, s5_log_dt, s5_b_re, s5_b_im,
           s5_c_re, s5_c_im, s5_d, s5_w_glu, norm_kv, w_kv, w_q, w_o, w_ffn_in, w_ffn_out,
           norm_final):
    hd = N_HEADS * HEAD_DIM
    row = lambda g: g.reshape(1, D_MODEL)
    bw, avec, cw = _s5_weights(s5_a_re[0], s5_a_im[0], s5_log_dt[0], s5_b_re[0], s5_b_im[0],
                               s5_c_re[0], s5_c_im[0])
    wglu = s5_w_glu[0].astype(BF16)
    w_in = w_ffn_in.astype(BF16)
    w_out = w_ffn_out.astype(BF16)
    wq = w_q[0].astype(BF16)
    wkv = w_kv.astype(BF16)
    wo_att = w_o[0].astype(BF16)

    s5 = functools.partial(_s5_layer, gain=row(norm_mix[0]), bw=bw, avec=avec, cw=cw,
                           d_skip=row(s5_d[0]), wglu=wglu)
    x_meta = jnp.broadcast_to(meta_tokens.astype(x.dtype)[None], (BATCH, N_META, D_MODEL))
    zero_state = jnp.zeros((S5_SLABS, BATCH, 2 * S5_SLAB_STATE), F32)
    h_meta, meta_state = s5(x_meta, init_state=zero_state, chunk=N_META)
    h_real, _ = s5(x, init_state=meta_state, chunk=S5_CHUNK)
    h_meta = h_meta[0]
    h_real = h_real.reshape(BATCH * SEQ, D_MODEL)

    ffn0 = functools.partial(_ffn_layer, gain=row(norm_ffn[0]), w_in=w_in, w_out=w_out, layer=0)
    h_meta = ffn0(h_meta, rows_per_step=N_META)
    h_real = ffn0(h_real, rows_per_step=FFN_ROWS)

    proj = functools.partial(_proj_layer, gq=row(norm_mix[1]), gkv=row(norm_kv),
                             wq=wq, wkv=wkv)
    _, k_meta, v_meta = proj(h_meta, rows_per_step=N_META)
    q, k, v = proj(h_real, rows_per_step=PROJ_ROWS)

    pad = ((0, META_PAD - N_META), (0, 0))
    seq = lambda a: a.reshape(BATCH, SEQ, hd)
    o = _attention(seq(q), seq(k), seq(v), jnp.pad(k_meta, pad), jnp.pad(v_meta, pad))

    out = _out_layer(h_real, o.reshape(BATCH * SEQ, hd), wo_att, row(norm_ffn[1]),
                     w_in, w_out, row(norm_final), layer=1, rows_per_step=FFN_ROWS)
    return out.reshape(BATCH, SEQ, D_MODEL)
```

```python
import functools
import math

import jax
import jax.numpy as jnp
from jax import lax
from jax.experimental import pallas as pl
from jax.experimental.pallas import tpu as pltpu

D_MODEL = 1024
BATCH = 8
SEQ = 4096
N_META = 16
S5_GROUP = 16
S5_GROUPS = 64
S5_STATE = 64
N_HEADS = 16
HEAD_DIM = 64
D_FF = 2816
RMS_EPS = 1e-6

LANES = 128
SUBLANES = 8
MXU_DIM = 256
VMEM_LIMIT_BYTES = 56 * 1024 * 1024

S5_SLAB_GROUPS = MXU_DIM // S5_GROUP
S5_SLABS = S5_GROUPS // S5_SLAB_GROUPS
S5_SLAB_STATE = S5_SLAB_GROUPS * S5_STATE
S5_CHUNK = 64

FFN_ROWS = 1024
FFN_F_CHUNKS = ((0, 1024), (1024, 2048), (2048, D_FF))
PROJ_ROWS = 1024

ATT_Q = 256
ATT_K = 256
ATT_GROUP = 8
ATT_HEADS = 2
ATT_LANES = ATT_HEADS * HEAD_DIM
META_PAD = 128
MASKED_LOGIT = -1e30
REM_FLOOR = -151.0
SOFTPLUS_CUT = 64.0
LOG2E = 1.4426950408889634

F32 = jnp.float32
BF16 = jnp.bfloat16


def _rmsnorm(x, gain):
    ms = jnp.mean(x * x, axis=-1, keepdims=True)
    return x * lax.rsqrt(ms + RMS_EPS) * gain


def _gelu_tanh(y):
    c = math.sqrt(2.0 / math.pi)
    return 0.5 * y * (1.0 + jnp.tanh(c * (y + 0.044715 * (y * y * y))))


def _sigmoid(x):
    return 1.0 / (1.0 + jnp.exp(-x))


def _const_spec(shape):
    nd = len(shape)
    return pl.BlockSpec(shape, lambda *_: (0,) * nd, pipeline_mode=pl.Buffered(1))


def _layer_spec(shape, layer):
    rest = (0,) * (len(shape) - 1)
    return pl.BlockSpec((None,) + tuple(shape[1:]), lambda *_: (layer,) + rest,
                        pipeline_mode=pl.Buffered(1))


def _s5_kernel(x_ref, gain_ref, bw_ref, avec_ref, cw_ref, d_ref, wglu_ref, init_ref,
               out_ref, fin_ref, state_ref, hs_ref, bu_ref, *, chunk):
    half = S5_SLAB_STATE
    lane_slabs = D_MODEL // LANES

    @pl.when(pl.program_id(0) == 0)
    def _():
        state_ref[...] = init_ref[...]

    for b in range(BATCH):
        for j in range(lane_slabs):
            hs_ref[j, pl.ds(b, chunk, stride=BATCH), :] = x_ref[b, :, j * LANES:(j + 1) * LANES]
    h = jnp.concatenate([hs_ref[j] for j in range(lane_slabs)], axis=1)
    u = _rmsnorm(h, gain_ref[...])
    ub = u.astype(BF16)
    zs = []
    for s in range(S5_SLABS):
        chans = slice(s * MXU_DIM, (s + 1) * MXU_DIM)
        bu_ref[...] = jnp.dot(ub[:, chans], bw_ref[s], preferred_element_type=F32)
        a_re = avec_ref[s, :, :half]
        a_im = avec_ref[s, :, half:]

        def step(l, carry, a_re=a_re, a_im=a_im):
            x_re, x_im = carry
            row = pl.multiple_of(l * SUBLANES, SUBLANES)
            b_re = bu_ref[pl.ds(row, SUBLANES), :half]
            b_im = bu_ref[pl.ds(row, SUBLANES), half:]
            n_re = a_re * x_re - a_im * x_im + b_re
            n_im = a_re * x_im + a_im * x_re + b_im
            bu_ref[pl.ds(row, SUBLANES), :half] = n_re
            bu_ref[pl.ds(row, SUBLANES), half:] = n_im
            return n_re, n_im

        x_re, x_im = lax.fori_loop(0, chunk, step,
                                   (state_ref[s, :, :half], state_ref[s, :, half:]), unroll=True)
        state_ref[s, :, :half] = x_re
        state_ref[s, :, half:] = x_im
        y = jnp.dot(bu_ref[...].astype(BF16), cw_ref[s], preferred_element_type=F32)
        zs.append(_gelu_tanh(y + d_ref[:, chans] * u[:, chans]).astype(BF16))

    vg = jnp.dot(jnp.concatenate(zs, axis=1), wglu_ref[...], preferred_element_type=F32)
    res = h + vg[:, :D_MODEL] * _sigmoid(vg[:, D_MODEL:])
    for j in range(lane_slabs):
        hs_ref[j] = res[:, j * LANES:(j + 1) * LANES]
    for b in range(BATCH):
        for j in range(lane_slabs):
            out_ref[b, :, j * LANES:(j + 1) * LANES] = hs_ref[j, pl.ds(b, chunk, stride=BATCH), :]

    @pl.when(pl.program_id(0) == pl.num_programs(0) - 1)
    def _():
        fin_ref[...] = state_ref[...]


def _s5_layer(x, gain, bw, avec, cw, d_skip, wglu, init_state, *, chunk):
    length = x.shape[1]
    m = chunk * BATCH
    assert length % chunk == 0
    state_shape = (S5_SLABS, BATCH, 2 * S5_SLAB_STATE)
    io_spec = pl.BlockSpec((BATCH, chunk, D_MODEL), lambda i: (0, i, 0))
    return pl.pallas_call(
        functools.partial(_s5_kernel, chunk=chunk),
        grid=(length // chunk,),
        in_specs=[
            io_spec,
            _const_spec((1, D_MODEL)),
            _const_spec(bw.shape),
            _const_spec(avec.shape),
            _const_spec(cw.shape),
            _const_spec((1, D_MODEL)),
            _const_spec(wglu.shape),
            _const_spec(state_shape),
        ],
        out_specs=[
            io_spec,
            pl.BlockSpec(state_shape, lambda i: (0, 0, 0)),
        ],
        out_shape=[
            jax.ShapeDtypeStruct(x.shape, F32),
            jax.ShapeDtypeStruct(state_shape, F32),
        ],
        scratch_shapes=[
            pltpu.VMEM(state_shape, F32),
            pltpu.VMEM((D_MODEL // LANES, m, LANES), F32),
            pltpu.VMEM((m, 2 * S5_SLAB_STATE), F32),
        ],
        compiler_params=pltpu.CompilerParams(
            dimension_semantics=("arbitrary",), vmem_limit_bytes=VMEM_LIMIT_BYTES),
        name="s5_layer",
    )(x, gain, bw, avec, cw, d_skip, wglu, init_state)


def _swiglu(hn, win_ref, wo_ref):
    acc = None
    for f0, f1 in FFN_F_CHUNKS:
        g = jnp.dot(hn, win_ref[:, f0:f1], preferred_element_type=F32)
        u = jnp.dot(hn, win_ref[:, D_FF + f0:D_FF + f1], preferred_element_type=F32)
        a = (g * _sigmoid(g) * u).astype(BF16)
        part = jnp.dot(a, wo_ref[f0:f1, :], preferred_element_type=F32)
        acc = part if acc is None else acc + part
    return acc


def _ffn_kernel(h_ref, gain_ref, win_ref, wo_ref, out_ref):
    h = h_ref[...]
    hn = _rmsnorm(h, gain_ref[...]).astype(BF16)
    out_ref[...] = h + _swiglu(hn, win_ref, wo_ref)


def _ffn_layer(h, gain, w_in, w_out, *, layer, rows_per_step):
    rows = h.shape[0]
    assert rows % rows_per_step == 0
    return pl.pallas_call(
        _ffn_kernel,
        grid=(rows // rows_per_step,),
        in_specs=[
            pl.BlockSpec((rows_per_step, D_MODEL), lambda i: (i, 0)),
            _const_spec((1, D_MODEL)),
            _layer_spec(w_in.shape, layer),
            _layer_spec(w_out.shape, layer),
        ],
        out_specs=pl.BlockSpec((rows_per_step, D_MODEL), lambda i: (i, 0)),
        out_shape=jax.ShapeDtypeStruct((rows, D_MODEL), F32),
        compiler_params=pltpu.CompilerParams(
            dimension_semantics=("parallel",), vmem_limit_bytes=VMEM_LIMIT_BYTES),
        name="ffn_layer",
    )(h, gain, w_in, w_out)


def _proj_kernel(h_ref, gq_ref, gkv_ref, wq_ref, wkv_ref, q_ref, k_ref, v_ref):
    h = h_ref[...]
    y = h * lax.rsqrt(jnp.mean(h * h, axis=-1, keepdims=True) + RMS_EPS)
    hq = (y * gq_ref[...]).astype(BF16)
    hkv = (y * gkv_ref[...]).astype(BF16)
    scale = LOG2E / math.sqrt(HEAD_DIM)
    q_ref[...] = (jnp.dot(hq, wq_ref[...], preferred_element_type=F32) * scale).astype(BF16)
    hd = q_ref.shape[1]
    k_ref[...] = jnp.dot(hkv, wkv_ref[:, :hd], preferred_element_type=F32).astype(BF16)
    v_ref[...] = jnp.dot(hkv, wkv_ref[:, hd:], preferred_element_type=F32).astype(BF16)


def _proj_layer(h, gq, gkv, wq, wkv, *, rows_per_step):
    rows = h.shape[0]
    assert rows % rows_per_step == 0
    hd = N_HEADS * HEAD_DIM
    row_spec = pl.BlockSpec((rows_per_step, hd), lambda i: (i, 0))
    return pl.pallas_call(
        _proj_kernel,
        grid=(rows // rows_per_step,),
        in_specs=[
            pl.BlockSpec((rows_per_step, D_MODEL), lambda i: (i, 0)),
            _const_spec((1, D_MODEL)),
            _const_spec((1, D_MODEL)),
            _const_spec(wq.shape),
            _const_spec(wkv.shape),
        ],
        out_specs=[row_spec, row_spec, row_spec],
        out_shape=[jax.ShapeDtypeStruct((rows, hd), BF16)] * 3,
        compiler_params=pltpu.CompilerParams(
            dimension_semantics=("parallel",), vmem_limit_bytes=VMEM_LIMIT_BYTES),
        name="qkv_proj",
    )(h, gq, gkv, wq, wkv)


def _attn_kernel(q_ref, k_ref, v_ref, km_ref, vm_ref, tri_ref, trim_ref, dbias_ref, mbias_ref,
                 o_ref, acc_ref, rem_ref):
    head_of_lane = lax.broadcasted_iota(jnp.int32, (1, ATT_LANES), 1) // HEAD_DIM

    def block(t, q_st, kb, vb, tri_neg, bias):
        nk = kb.shape[0]
        z = lax.dot_general(q_st, kb, (((1,), (1,)), ((), ())),
                            preferred_element_type=F32)
        if bias is not None:
            z = z + bias
        sp = jnp.where(z > SOFTPLUS_CUT, z, jnp.log2(1.0 + jnp.exp2(z)))
        suffix = jnp.dot(sp.astype(BF16), tri_neg, preferred_element_type=F32)
        rem = rem_ref[t]
        rem_b = jnp.concatenate([rem] * (nk // LANES), axis=1)
        w = jnp.exp2((z - sp) + suffix + rem_b)
        acc_ref[t] += jnp.dot(w.astype(BF16), vb, preferred_element_type=F32)
        rem_ref[t] = rem - jnp.sum(sp, axis=1, keepdims=True)

    def kv_block(j):
        k0 = pl.multiple_of(j * ATT_K, ATT_K)
        return k_ref[pl.ds(k0, ATT_K), :], v_ref[pl.ds(k0, ATT_K), :]

    def stacked_q(qi):
        q = q_ref[pl.ds(pl.multiple_of(qi * ATT_Q, ATT_Q), ATT_Q), :]
        zero = jnp.zeros_like(q)
        return jnp.concatenate([jnp.where(head_of_lane == hd, q, zero) for hd in range(ATT_HEADS)],
                               axis=0)

    def live_flag(t):
        return (jnp.max(rem_ref[t]) >= REM_FLOOR).astype(jnp.int32)

    def finish(t, qi, q_st, live0, j0):
        def earlier(c):
            j, _ = c
            block(t, q_st, *kv_block(j), tri_ref[...], None)
            return j - 1, live_flag(t)

        _, live = lax.while_loop(lambda c: jnp.logical_and(c[0] >= 0, c[1] > 0), earlier,
                                 (j0, live0))

        @pl.when(live > 0)
        def _():
            block(t, q_st, km_ref[...], vm_ref[...], trim_ref[...], mbias_ref[...])

        o_t = acc_ref[t, :ATT_Q]
        for hd in range(1, ATT_HEADS):
            o_t = jnp.where(head_of_lane == hd, acc_ref[t, hd * ATT_Q:(hd + 1) * ATT_Q], o_t)
        o_ref[pl.ds(pl.multiple_of(qi * ATT_Q, ATT_Q), ATT_Q), :] = o_t.astype(o_ref.dtype)

    def tile_group(q_first, first):
        tiles = [q_first + t for t in range(ATT_GROUP)]
        qs = [stacked_q(qi) for qi in tiles]
        kvs = [kv_block(qi) for qi in tiles]
        acc_ref[...] = jnp.zeros_like(acc_ref)
        rem_ref[...] = jnp.zeros_like(rem_ref)
        for t in range(ATT_GROUP):
            block(t, qs[t], *kvs[t], tri_ref[...], dbias_ref[...])
            if t > 0:
                block(t, qs[t], *kvs[t - 1], tri_ref[...], None)
            elif not first:
                block(t, qs[t], *kv_block(q_first - 1), tri_ref[...], None)
        lives = [jnp.int32(1) if (first and t == 0) else live_flag(t) for t in range(ATT_GROUP)]
        for t in range(ATT_GROUP):
            finish(t, tiles[t], qs[t], lives[t], tiles[t] - 2)

    tile_group(0, True)

    def later_group(i, carry):
        tile_group(ATT_GROUP * i, False)
        return carry

    lax.fori_loop(1, SEQ // (ATT_GROUP * ATT_Q), later_group, 0)


def _attention(q, k, v, k_meta, v_meta):
    pairs = N_HEADS // ATT_HEADS
    seq_spec = pl.BlockSpec((None, SEQ, ATT_LANES), lambda b, p: (b, 0, p))
    meta_spec = pl.BlockSpec((META_PAD, ATT_LANES), lambda b, p: (0, p))
    tri = _suffix_matrix(ATT_K)
    tri_meta = _suffix_matrix(META_PAD)
    row = lax.broadcasted_iota(jnp.int32, (ATT_Q, ATT_K), 0)
    col = lax.broadcasted_iota(jnp.int32, (ATT_Q, ATT_K), 1)
    diag_bias = jnp.where(col < row, 0.0, MASKED_LOGIT).astype(F32)
    diag_bias = jnp.concatenate([diag_bias] * ATT_HEADS, axis=0)
    mcol = lax.broadcasted_iota(jnp.int32, (1, META_PAD), 1)
    meta_bias = jnp.where(mcol < N_META, 0.0, MASKED_LOGIT).astype(F32)
    return pl.pallas_call(
        _attn_kernel,
        grid=(BATCH, pairs),
        in_specs=[
            seq_spec, seq_spec, seq_spec, meta_spec, meta_spec,
            _const_spec(tri.shape), _const_spec(tri_meta.shape),
            _const_spec(diag_bias.shape), _const_spec(meta_bias.shape),
        ],
        out_specs=seq_spec,
        out_shape=jax.ShapeDtypeStruct(q.shape, BF16),
        scratch_shapes=[
            pltpu.VMEM((ATT_GROUP, ATT_HEADS * ATT_Q, ATT_LANES), F32),
            pltpu.VMEM((ATT_GROUP, ATT_HEADS * ATT_Q, LANES), F32),
        ],
        compiler_params=pltpu.CompilerParams(
            dimension_semantics=("parallel", "parallel"), vmem_limit_bytes=VMEM_LIMIT_BYTES),
        name="stickbreak_attn",
    )(q, k, v, k_meta, v_meta, tri, tri_meta, diag_bias, meta_bias)


def _out_kernel(h_ref, o_ref, wo_att_ref, gain_ref, win_ref, wo_ref, gfin_ref, out_ref):
    h = h_ref[...] + jnp.dot(o_ref[...], wo_att_ref[...], preferred_element_type=F32)
    hn = _rmsnorm(h, gain_ref[...]).astype(BF16)
    h = h + _swiglu(hn, win_ref, wo_ref)
    out_ref[...] = _rmsnorm(h, gfin_ref[...])


def _out_layer(h, o, wo_att, gain, w_in, w_out, gfin, *, layer, rows_per_step):
    rows = h.shape[0]
    assert rows % rows_per_step == 0
    row_spec = pl.BlockSpec((rows_per_step, D_MODEL), lambda i: (i, 0))
    return pl.pallas_call(
        _out_kernel,
        grid=(rows // rows_per_step,),
        in_specs=[
            row_spec,
            pl.BlockSpec((rows_per_step, o.shape[1]), lambda i: (i, 0)),
            _const_spec(wo_att.shape),
            _const_spec((1, D_MODEL)),
            _layer_spec(w_in.shape, layer),
            _layer_spec(w_out.shape, layer),
            _const_spec((1, D_MODEL)),
        ],
        out_specs=row_spec,
        out_shape=jax.ShapeDtypeStruct((rows, D_MODEL), F32),
        compiler_params=pltpu.CompilerParams(
            dimension_semantics=("parallel",), vmem_limit_bytes=VMEM_LIMIT_BYTES),
        name="oproj_ffn_final",
    )(h, o, wo_att, gain, w_in, w_out, gfin)


def _s5_weights(a_re, a_im, log_dt, b_re, b_im, c_re, c_im):
    dt = jnp.exp(log_dt)[:, None]
    mag = jnp.exp(dt * a_re)
    ang = dt * a_im
    abar_re = mag * jnp.cos(ang)
    abar_im = mag * jnp.sin(ang)
    den = a_re * a_re + a_im * a_im
    coef_re = ((abar_re - 1.0) * a_re + abar_im * a_im) / den
    coef_im = (abar_im * a_re - (abar_re - 1.0) * a_im) / den
    bbar_re = coef_re[..., None] * b_re - coef_im[..., None] * b_im
    bbar_im = coef_re[..., None] * b_im + coef_im[..., None] * b_re
    same_group = jnp.eye(S5_SLAB_GROUPS, dtype=bool)[None, :, None, :, None]
    slab = (S5_SLABS, S5_SLAB_GROUPS)

    def block_diag(w):
        full = jnp.where(same_group, w[:, :, :, None, :], 0.0)
        return full.reshape(S5_SLABS, S5_SLAB_GROUPS * w.shape[2], S5_SLAB_GROUPS * w.shape[3])

    def b_slab(bb):
        return block_diag(jnp.swapaxes(bb.reshape(slab + (S5_STATE, S5_GROUP)), 2, 3))

    def c_slab(cc):
        return block_diag(jnp.swapaxes(cc.reshape(slab + (S5_GROUP, S5_STATE)), 2, 3))

    bw = jnp.concatenate([b_slab(bbar_re), b_slab(bbar_im)], axis=2).astype(BF16)
    cw = jnp.concatenate([c_slab(c_re), -c_slab(c_im)], axis=1).astype(BF16)
    avec = jnp.concatenate([abar_re.reshape(S5_SLABS, S5_SLAB_STATE),
                            abar_im.reshape(S5_SLABS, S5_SLAB_STATE)], axis=1)
    avec = jnp.broadcast_to(avec[:, None, :], (S5_SLABS, BATCH, 2 * S5_SLAB_STATE))
    return bw, avec, cw


def _suffix_matrix(n):
    j = lax.broadcasted_iota(jnp.int32, (n, n), 0)
    s = lax.broadcasted_iota(jnp.int32, (n, n), 1)
    return jnp.where(j > s, -1.0, 0.0).astype(BF16)


def kernel(x, meta_tokens, norm_mix, norm_ffn, s5_a_re, s5_a_im, s5_log_dt, s5_b_re, s5_b_im,
           s5_c_re, s5_c_im, s5_d, s5_w_glu, norm_kv, w_kv, w_q, w_o, w_ffn_in, w_ffn_out,
           norm_final):
    hd = N_HEADS * HEAD_DIM
    row = lambda g: g.reshape(1, D_MODEL)
    bw, avec, cw = _s5_weights(s5_a_re[0], s5_a_im[0], s5_log_dt[0], s5_b_re[0], s5_b_im[0],
                               s5_c_re[0], s5_c_im[0])
    wglu = s5_w_glu[0].astype(BF16)
    w_in = w_ffn_in.astype(BF16)
    w_out = w_ffn_out.astype(BF16)
    wq = w_q[0].astype(BF16)
    wkv = w_kv.astype(BF16)
    wo_att = w_o[0].astype(BF16)

    s5 = functools.partial(_s5_layer, gain=row(norm_mix[0]), bw=bw, avec=avec, cw=cw,
                           d_skip=row(s5_d[0]), wglu=wglu)
    x_meta = jnp.broadcast_to(meta_tokens.astype(x.dtype)[None], (BATCH, N_META, D_MODEL))
    zero_state = jnp.zeros((S5_SLABS, BATCH, 2 * S5_SLAB_STATE), F32)
    h_meta, meta_state = s5(x_meta, init_state=zero_state, chunk=N_META)
    h_real, _ = s5(x, init_state=meta_state, chunk=S5_CHUNK)
    h_meta = h_meta[0]
    h_real = h_real.reshape(BATCH * SEQ, D_MODEL)

    ffn0 = functools.partial(_ffn_layer, gain=row(norm_ffn[0]), w_in=w_in, w_out=w_out, layer=0)
    h_meta = ffn0(h_meta, rows_per_step=N_META)
    h_real = ffn0(h_real, rows_per_step=FFN_ROWS)

    proj = functools.partial(_proj_layer, gq=row(norm_mix[1]), gkv=row(norm_kv),
                             wq=wq, wkv=wkv)
    _, k_meta, v_meta = proj(h_meta, rows_per_step=N_META)
    q, k, v = proj(h_real, rows_per_step=PROJ_ROWS)

    pad = ((0, META_PAD - N_META), (0, 0))
    seq = lambda a: a.reshape(BATCH, SEQ, hd)
    o = _attention(seq(q), seq(k), seq(v), jnp.pad(k_meta, pad), jnp.pad(v_meta, pad))

    out = _out_layer(h_real, o.reshape(BATCH * SEQ, hd), wo_att, row(norm_ffn[1]),
                     w_in, w_out, row(norm_final), layer=1, rows_per_step=FFN_ROWS)
    return out.reshape(BATCH, SEQ, D_MODEL)
```

```python
import functools
import math

import jax
import jax.numpy as jnp
from jax import lax
from jax.experimental import pallas as pl
from jax.experimental.pallas import tpu as pltpu

D_MODEL = 1024
BATCH = 8
SEQ = 4096
N_META = 16
S5_GROUP = 16
S5_GROUPS = 64
S5_STATE = 64
N_HEADS = 16
HEAD_DIM = 64
D_FF = 2816
RMS_EPS = 1e-6

LANES = 128
SUBLANES = 8
MXU_DIM = 256
VMEM_LIMIT_BYTES = 56 * 1024 * 1024

S5_SLAB_GROUPS = MXU_DIM // S5_GROUP
S5_SLABS = S5_GROUPS // S5_SLAB_GROUPS
S5_SLAB_STATE = S5_SLAB_GROUPS * S5_STATE
S5_CHUNK = 64

FFN_ROWS = 1024
FFN_F_CHUNKS = ((0, 1024), (1024, 2048), (2048, D_FF))
PROJ_ROWS = 1024

ATT_Q = 256
ATT_K = 256
ATT_GROUP = 16
ATT_HEADS = 2
ATT_LANES = ATT_HEADS * HEAD_DIM
META_PAD = 128
MASKED_LOGIT = -1e30
REM_FLOOR = -151.0
SOFTPLUS_CUT = 64.0
LOG2E = 1.4426950408889634

F32 = jnp.float32
BF16 = jnp.bfloat16


def _rmsnorm(x, gain):
    ms = jnp.mean(x * x, axis=-1, keepdims=True)
    return x * lax.rsqrt(ms + RMS_EPS) * gain


def _gelu_tanh(y):
    c = math.sqrt(2.0 / math.pi)
    return 0.5 * y * (1.0 + jnp.tanh(c * (y + 0.044715 * (y * y * y))))


def _sigmoid(x):
    return 1.0 / (1.0 + jnp.exp(-x))


def _const_spec(shape):
    nd = len(shape)
    return pl.BlockSpec(shape, lambda *_: (0,) * nd, pipeline_mode=pl.Buffered(1))


def _layer_spec(shape, layer):
    rest = (0,) * (len(shape) - 1)
    return pl.BlockSpec((None,) + tuple(shape[1:]), lambda *_: (layer,) + rest,
                        pipeline_mode=pl.Buffered(1))


def _s5_kernel(x_ref, gain_ref, bw_ref, avec_ref, cw_ref, d_ref, wglu_ref, init_ref,
               out_ref, fin_ref, state_ref, hs_ref, bu_ref, *, chunk):
    half = S5_SLAB_STATE
    lane_slabs = D_MODEL // LANES

    @pl.when(pl.program_id(0) == 0)
    def _():
        state_ref[...] = init_ref[...]

    for b in range(BATCH):
        for j in range(lane_slabs):
            hs_ref[j, pl.ds(b, chunk, stride=BATCH), :] = x_ref[b, :, j * LANES:(j + 1) * LANES]
    h = jnp.concatenate([hs_ref[j] for j in range(lane_slabs)], axis=1)
    u = _rmsnorm(h, gain_ref[...])
    ub = u.astype(BF16)
    zs = []
    for s in range(S5_SLABS):
        chans = slice(s * MXU_DIM, (s + 1) * MXU_DIM)
        bu_ref[...] = jnp.dot(ub[:, chans], bw_ref[s], preferred_element_type=F32)
        a_re = avec_ref[s, :, :half]
        a_im = avec_ref[s, :, half:]

        def step(l, carry, a_re=a_re, a_im=a_im):
            x_re, x_im = carry
            row = pl.multiple_of(l * SUBLANES, SUBLANES)
            b_re = bu_ref[pl.ds(row, SUBLANES), :half]
            b_im = bu_ref[pl.ds(row, SUBLANES), half:]
            n_re = a_re * x_re - a_im * x_im + b_re
            n_im = a_re * x_im + a_im * x_re + b_im
            bu_ref[pl.ds(row, SUBLANES), :half] = n_re
            bu_ref[pl.ds(row, SUBLANES), half:] = n_im
            return n_re, n_im

        x_re, x_im = lax.fori_loop(0, chunk, step,
                                   (state_ref[s, :, :half], state_ref[s, :, half:]), unroll=True)
        state_ref[s, :, :half] = x_re
        state_ref[s, :, half:] = x_im
        y = jnp.dot(bu_ref[...].astype(BF16), cw_ref[s], preferred_element_type=F32)
        zs.append(_gelu_tanh(y + d_ref[:, chans] * u[:, chans]).astype(BF16))

    vg = jnp.dot(jnp.concatenate(zs, axis=1), wglu_ref[...], preferred_element_type=F32)
    res = h + vg[:, :D_MODEL] * _sigmoid(vg[:, D_MODEL:])
    for j in range(lane_slabs):
        hs_ref[j] = res[:, j * LANES:(j + 1) * LANES]
    for b in range(BATCH):
        for j in range(lane_slabs):
            out_ref[b, :, j * LANES:(j + 1) * LANES] = hs_ref[j, pl.ds(b, chunk, stride=BATCH), :]

    @pl.when(pl.program_id(0) == pl.num_programs(0) - 1)
    def _():
        fin_ref[...] = state_ref[...]


def _s5_layer(x, gain, bw, avec, cw, d_skip, wglu, init_state, *, chunk):
    length = x.shape[1]
    m = chunk * BATCH
    assert length % chunk == 0
    state_shape = (S5_SLABS, BATCH, 2 * S5_SLAB_STATE)
    io_spec = pl.BlockSpec((BATCH, chunk, D_MODEL), lambda i: (0, i, 0))
    return pl.pallas_call(
        functools.partial(_s5_kernel, chunk=chunk),
        grid=(length // chunk,),
        in_specs=[
            io_spec,
            _const_spec((1, D_MODEL)),
            _const_spec(bw.shape),
            _const_spec(avec.shape),
            _const_spec(cw.shape),
            _const_spec((1, D_MODEL)),
            _const_spec(wglu.shape),
            _const_spec(state_shape),
        ],
        out_specs=[
            io_spec,
            pl.BlockSpec(state_shape, lambda i: (0, 0, 0)),
        ],
        out_shape=[
            jax.ShapeDtypeStruct(x.shape, F32),
            jax.ShapeDtypeStruct(state_shape, F32),
        ],
        scratch_shapes=[
            pltpu.VMEM(state_shape, F32),
            pltpu.VMEM((D_MODEL // LANES, m, LANES), F32),
            pltpu.VMEM((m, 2 * S5_SLAB_STATE), F32),
        ],
        compiler_params=pltpu.CompilerParams(
            dimension_semantics=("arbitrary",), vmem_limit_bytes=VMEM_LIMIT_BYTES),
        name="s5_layer",
    )(x, gain, bw, avec, cw, d_skip, wglu, init_state)


def _swiglu(hn, win_ref, wo_ref):
    acc = None
    for f0, f1 in FFN_F_CHUNKS:
        g = jnp.dot(hn, win_ref[:, f0:f1], preferred_element_type=F32)
        u = jnp.dot(hn, win_ref[:, D_FF + f0:D_FF + f1], preferred_element_type=F32)
        a = (g * _sigmoid(g) * u).astype(BF16)
        part = jnp.dot(a, wo_ref[f0:f1, :], preferred_element_type=F32)
        acc = part if acc is None else acc + part
    return acc


def _ffn_kernel(h_ref, gain_ref, win_ref, wo_ref, out_ref):
    h = h_ref[...]
    hn = _rmsnorm(h, gain_ref[...]).astype(BF16)
    out_ref[...] = h + _swiglu(hn, win_ref, wo_ref)


def _ffn_layer(h, gain, w_in, w_out, *, layer, rows_per_step):
    rows = h.shape[0]
    assert rows % rows_per_step == 0
    return pl.pallas_call(
        _ffn_kernel,
        grid=(rows // rows_per_step,),
        in_specs=[
            pl.BlockSpec((rows_per_step, D_MODEL), lambda i: (i, 0)),
            _const_spec((1, D_MODEL)),
            _layer_spec(w_in.shape, layer),
            _layer_spec(w_out.shape, layer),
        ],
        out_specs=pl.BlockSpec((rows_per_step, D_MODEL), lambda i: (i, 0)),
        out_shape=jax.ShapeDtypeStruct((rows, D_MODEL), F32),
        compiler_params=pltpu.CompilerParams(
            dimension_semantics=("parallel",), vmem_limit_bytes=VMEM_LIMIT_BYTES),
        name="ffn_layer",
    )(h, gain, w_in, w_out)


def _proj_kernel(h_ref, gq_ref, gkv_ref, wq_ref, wkv_ref, q_ref, k_ref, v_ref):
    h = h_ref[...]
    y = h * lax.rsqrt(jnp.mean(h * h, axis=-1, keepdims=True) + RMS_EPS)
    hq = (y * gq_ref[...]).astype(BF16)
    hkv = (y * gkv_ref[...]).astype(BF16)
    scale = LOG2E / math.sqrt(HEAD_DIM)
    q_ref[...] = (jnp.dot(hq, wq_ref[...], preferred_element_type=F32) * scale).astype(BF16)
    hd = q_ref.shape[1]
    k_ref[...] = jnp.dot(hkv, wkv_ref[:, :hd], preferred_element_type=F32).astype(BF16)
    v_ref[...] = jnp.dot(hkv, wkv_ref[:, hd:], preferred_element_type=F32).astype(BF16)


def _proj_layer(h, gq, gkv, wq, wkv, *, rows_per_step):
    rows = h.shape[0]
    assert rows % rows_per_step == 0
    hd = N_HEADS * HEAD_DIM
    row_spec = pl.BlockSpec((rows_per_step, hd), lambda i: (i, 0))
    return pl.pallas_call(
        _proj_kernel,
        grid=(rows // rows_per_step,),
        in_specs=[
            pl.BlockSpec((rows_per_step, D_MODEL), lambda i: (i, 0)),
            _const_spec((1, D_MODEL)),
            _const_spec((1, D_MODEL)),
            _const_spec(wq.shape),
            _const_spec(wkv.shape),
        ],
        out_specs=[row_spec, row_spec, row_spec],
        out_shape=[jax.ShapeDtypeStruct((rows, hd), BF16)] * 3,
        compiler_params=pltpu.CompilerParams(
            dimension_semantics=("parallel",), vmem_limit_bytes=VMEM_LIMIT_BYTES),
        name="qkv_proj",
    )(h, gq, gkv, wq, wkv)


def _attn_kernel(q_ref, k_ref, v_ref, km_ref, vm_ref, tri_ref, trim_ref, dbias_ref, mbias_ref,
                 o_ref, acc_ref, rem_ref):
    head_of_lane = lax.broadcasted_iota(jnp.int32, (1, ATT_LANES), 1) // HEAD_DIM

    def block(t, q_st, kb, vb, tri_neg, bias):
        nk = kb.shape[0]
        z = lax.dot_general(q_st, kb, (((1,), (1,)), ((), ())),
                            preferred_element_type=F32)
        if bias is not None:
            z = z + bias
        sp = jnp.where(z > SOFTPLUS_CUT, z, jnp.log2(1.0 + jnp.exp2(z)))
        suffix = jnp.dot(sp.astype(BF16), tri_neg, preferred_element_type=F32)
        rem = rem_ref[t]
        rem_b = jnp.concatenate([rem] * (nk // LANES), axis=1)
        w = jnp.exp2((z - sp) + suffix + rem_b)
        acc_ref[t] += jnp.dot(w.astype(BF16), vb, preferred_element_type=F32)
        rem_ref[t] = rem - jnp.sum(sp, axis=1, keepdims=True)

    def kv_block(j):
        k0 = pl.multiple_of(j * ATT_K, ATT_K)
        return k_ref[pl.ds(k0, ATT_K), :], v_ref[pl.ds(k0, ATT_K), :]

    def stacked_q(qi):
        q = q_ref[pl.ds(pl.multiple_of(qi * ATT_Q, ATT_Q), ATT_Q), :]
        zero = jnp.zeros_like(q)
        return jnp.concatenate([jnp.where(head_of_lane == hd, q, zero) for hd in range(ATT_HEADS)],
                               axis=0)

    def live_flag(t):
        return (jnp.max(rem_ref[t]) >= REM_FLOOR).astype(jnp.int32)

    def finish(t, qi, q_st, live0, j0):
        def earlier(c):
            j, _ = c
            block(t, q_st, *kv_block(j), tri_ref[...], None)
            return j - 1, live_flag(t)

        _, live = lax.while_loop(lambda c: jnp.logical_and(c[0] >= 0, c[1] > 0), earlier,
                                 (j0, live0))

        @pl.when(live > 0)
        def _():
            block(t, q_st, km_ref[...], vm_ref[...], trim_ref[...], mbias_ref[...])

        o_t = acc_ref[t, :ATT_Q]
        for hd in range(1, ATT_HEADS):
            o_t = jnp.where(head_of_lane == hd, acc_ref[t, hd * ATT_Q:(hd + 1) * ATT_Q], o_t)
        o_ref[pl.ds(pl.multiple_of(qi * ATT_Q, ATT_Q), ATT_Q), :] = o_t.astype(o_ref.dtype)

    def tile_group(q_first, first):
        tiles = [q_first + t for t in range(ATT_GROUP)]
        qs = [stacked_q(qi) for qi in tiles]
        kvs = [kv_block(qi) for qi in tiles]
        acc_ref[...] = jnp.zeros_like(acc_ref)
        rem_ref[...] = jnp.zeros_like(rem_ref)
        for t in range(ATT_GROUP):
            block(t, qs[t], *kvs[t], tri_ref[...], dbias_ref[...])
            if t > 0:
                block(t, qs[t], *kvs[t - 1], tri_ref[...], None)
            elif not first:
                block(t, qs[t], *kv_block(q_first - 1), tri_ref[...], None)
        lives = [jnp.int32(1) if (first and t == 0) else live_flag(t) for t in range(ATT_GROUP)]
        for t in range(ATT_GROUP):
            finish(t, tiles[t], qs[t], lives[t], tiles[t] - 2)

    tile_group(0, True)
    groups = SEQ // (ATT_GROUP * ATT_Q)
    if groups > 1:
        def later_group(i, carry):
            tile_group(ATT_GROUP * i, False)
            return carry

        lax.fori_loop(1, groups, later_group, 0)


def _attention(q, k, v, k_meta, v_meta):
    pairs = N_HEADS // ATT_HEADS
    seq_spec = pl.BlockSpec((None, SEQ, ATT_LANES), lambda b, p: (b, 0, p))
    meta_spec = pl.BlockSpec((META_PAD, ATT_LANES), lambda b, p: (0, p))
    tri = _suffix_matrix(ATT_K)
    tri_meta = _suffix_matrix(META_PAD)
    row = lax.broadcasted_iota(jnp.int32, (ATT_Q, ATT_K), 0)
    col = lax.broadcasted_iota(jnp.int32, (ATT_Q, ATT_K), 1)
    diag_bias = jnp.where(col < row, 0.0, MASKED_LOGIT).astype(F32)
    diag_bias = jnp.concatenate([diag_bias] * ATT_HEADS, axis=0)
    mcol = lax.broadcasted_iota(jnp.int32, (1, META_PAD), 1)
    meta_bias = jnp.where(mcol < N_META, 0.0, MASKED_LOGIT).astype(F32)
    return pl.pallas_call(
        _attn_kernel,
        grid=(BATCH, pairs),
        in_specs=[
            seq_spec, seq_spec, seq_spec, meta_spec, meta_spec,
            _const_spec(tri.shape), _const_spec(tri_meta.shape),
            _const_spec(diag_bias.shape), _const_spec(meta_bias.shape),
        ],
        out_specs=seq_spec,
        out_shape=jax.ShapeDtypeStruct(q.shape, BF16),
        scratch_shapes=[
            pltpu.VMEM((ATT_GROUP, ATT_HEADS * ATT_Q, ATT_LANES), F32),
            pltpu.VMEM((ATT_GROUP, ATT_HEADS * ATT_Q, LANES), F32),
        ],
        compiler_params=pltpu.CompilerParams(
            dimension_semantics=("parallel", "parallel"), vmem_limit_bytes=VMEM_LIMIT_BYTES),
        name="stickbreak_attn",
    )(q, k, v, k_meta, v_meta, tri, tri_meta, diag_bias, meta_bias)


def _out_kernel(h_ref, o_ref, wo_att_ref, gain_ref, win_ref, wo_ref, gfin_ref, out_ref):
    h = h_ref[...] + jnp.dot(o_ref[...], wo_att_ref[...], preferred_element_type=F32)
    hn = _rmsnorm(h, gain_ref[...]).astype(BF16)
    h = h + _swiglu(hn, win_ref, wo_ref)
    out_ref[...] = _rmsnorm(h, gfin_ref[...])


def _out_layer(h, o, wo_att, gain, w_in, w_out, gfin, *, layer, rows_per_step):
    rows = h.shape[0]
    assert rows % rows_per_step == 0
    row_spec = pl.BlockSpec((rows_per_step, D_MODEL), lambda i: (i, 0))
    return pl.pallas_call(
        _out_kernel,
        grid=(rows // rows_per_step,),
        in_specs=[
            row_spec,
            pl.BlockSpec((rows_per_step, o.shape[1]), lambda i: (i, 0)),
            _const_spec(wo_att.shape),
            _const_spec((1, D_MODEL)),
            _layer_spec(w_in.shape, layer),
            _layer_spec(w_out.shape, layer),
            _const_spec((1, D_MODEL)),
        ],
        out_specs=row_spec,
        out_shape=jax.ShapeDtypeStruct((rows, D_MODEL), F32),
        compiler_params=pltpu.CompilerParams(
            dimension_semantics=("parallel",), vmem_limit_bytes=VMEM_LIMIT_BYTES),
        name="oproj_ffn_final",
    )(h, o, wo_att, gain, w_in, w_out, gfin)


def _s5_weights(a_re, a_im, log_dt, b_re, b_im, c_re, c_im):
    dt = jnp.exp(log_dt)[:, None]
    mag = jnp.exp(dt * a_re)
    ang = dt * a_im
    abar_re = mag * jnp.cos(ang)
    abar_im = mag * jnp.sin(ang)
    den = a_re * a_re + a_im * a_im
    coef_re = ((abar_re - 1.0) * a_re + abar_im * a_im) / den
    coef_im = (abar_im * a_re - (abar_re - 1.0) * a_im) / den
    bbar_re = coef_re[..., None] * b_re - coef_im[..., None] * b_im
    bbar_im = coef_re[..., None] * b_im + coef_im[..., None] * b_re
    same_group = jnp.eye(S5_SLAB_GROUPS, dtype=bool)[None, :, None, :, None]
    slab = (S5_SLABS, S5_SLAB_GROUPS)

    def block_diag(w):
        full = jnp.where(same_group, w[:, :, :, None, :], 0.0)
        return full.reshape(S5_SLABS, S5_SLAB_GROUPS * w.shape[2], S5_SLAB_GROUPS * w.shape[3])

    def b_slab(bb):
        return block_diag(jnp.swapaxes(bb.reshape(slab + (S5_STATE, S5_GROUP)), 2, 3))

    def c_slab(cc):
        return block_diag(jnp.swapaxes(cc.reshape(slab + (S5_GROUP, S5_STATE)), 2, 3))

    bw = jnp.concatenate([b_slab(bbar_re), b_slab(bbar_im)], axis=2).astype(BF16)
    cw = jnp.concatenate([c_slab(c_re), -c_slab(c_im)], axis=1).astype(BF16)
    avec = jnp.concatenate([abar_re.reshape(S5_SLABS, S5_SLAB_STATE),
                            abar_im.reshape(S5_SLABS, S5_SLAB_STATE)], axis=1)
    avec = jnp.broadcast_to(avec[:, None, :], (S5_SLABS, BATCH, 2 * S5_SLAB_STATE))
    return bw, avec, cw


def _suffix_matrix(n):
    j = lax.broadcasted_iota(jnp.int32, (n, n), 0)
    s = lax.broadcasted_iota(jnp.int32, (n, n), 1)
    return jnp.where(j > s, -1.0, 0.0).astype(BF16)


def kernel(x, meta_tokens, norm_mix, norm_ffn, s5_a_re, s5_a_im, s5_log_dt, s5_b_re, s5_b_im,
           s5_c_re, s5_c_im, s5_d, s5_w_glu, norm_kv, w_kv, w_q, w_o, w_ffn_in, w_ffn_out,
           norm_final):
    hd = N_HEADS * HEAD_DIM
    row = lambda g: g.reshape(1, D_MODEL)
    bw, avec, cw = _s5_weights(s5_a_re[0], s5_a_im[0], s5_log_dt[0], s5_b_re[0], s5_b_im[0],
                               s5_c_re[0], s5_c_im[0])
    wglu = s5_w_glu[0].astype(BF16)
    w_in = w_ffn_in.astype(BF16)
    w_out = w_ffn_out.astype(BF16)
    wq = w_q[0].astype(BF16)
    wkv = w_kv.astype(BF16)
    wo_att = w_o[0].astype(BF16)

    s5 = functools.partial(_s5_layer, gain=row(norm_mix[0]), bw=bw, avec=avec, cw=cw,
                           d_skip=row(s5_d[0]), wglu=wglu)
    x_meta = jnp.broadcast_to(meta_tokens.astype(x.dtype)[None], (BATCH, N_META, D_MODEL))
    zero_state = jnp.zeros((S5_SLABS, BATCH, 2 * S5_SLAB_STATE), F32)
    h_meta, meta_state = s5(x_meta, init_state=zero_state, chunk=N_META)
    h_real, _ = s5(x, init_state=meta_state, chunk=S5_CHUNK)
    h_meta = h_meta[0]
    h_real = h_real.reshape(BATCH * SEQ, D_MODEL)

    ffn0 = functools.partial(_ffn_layer, gain=row(norm_ffn[0]), w_in=w_in, w_out=w_out, layer=0)
    h_meta = ffn0(h_meta, rows_per_step=N_META)
    h_real = ffn0(h_real, rows_per_step=FFN_ROWS)

    proj = functools.partial(_proj_layer, gq=row(norm_mix[1]), gkv=row(norm_kv),
                             wq=wq, wkv=wkv)
    _, k_meta, v_meta = proj(h_meta, rows_per_step=N_META)
    q, k, v = proj(h_real, rows_per_step=PROJ_ROWS)

    pad = ((0, META_PAD - N_META), (0, 0))
    seq = lambda a: a.reshape(BATCH, SEQ, hd)
    o = _attention(seq(q), seq(k), seq(v), jnp.pad(k_meta, pad), jnp.pad(v_meta, pad))

    out = _out_layer(h_real, o.reshape(BATCH * SEQ, hd), wo_att, row(norm_ffn[1]),
                     w_in, w_out, row(norm_final), layer=1, rows_per_step=FFN_ROWS)
    return out.reshape(BATCH, SEQ, D_MODEL)
```

```python
import functools
import math

import jax
import jax.numpy as jnp
from jax import lax
from jax.experimental import pallas as pl
from jax.experimental.pallas import tpu as pltpu

D_MODEL = 1024
BATCH = 8
SEQ = 4096
N_META = 16
S5_GROUP = 16
S5_GROUPS = 64
S5_STATE = 64
N_HEADS = 16
HEAD_DIM = 64
D_FF = 2816
RMS_EPS = 1e-6

LANES = 128
SUBLANES = 8
MXU_DIM = 256
VMEM_LIMIT_BYTES = 56 * 1024 * 1024

S5_SLAB_GROUPS = MXU_DIM // S5_GROUP
S5_SLABS = S5_GROUPS // S5_SLAB_GROUPS
S5_SLAB_STATE = S5_SLAB_GROUPS * S5_STATE
S5_CHUNK = 128

FFN_ROWS = 1024
FFN_F_CHUNKS = ((0, 1024), (1024, 2048), (2048, D_FF))
PROJ_ROWS = 1024

ATT_Q = 256
ATT_K = 256
ATT_GROUP = 16
ATT_HEADS = 2
ATT_LANES = ATT_HEADS * HEAD_DIM
META_PAD = 128
MASKED_LOGIT = -1e30
REM_FLOOR = -151.0
SOFTPLUS_CUT = 64.0
LOG2E = 1.4426950408889634

F32 = jnp.float32
BF16 = jnp.bfloat16


def _rmsnorm(x, gain):
    ms = jnp.mean(x * x, axis=-1, keepdims=True)
    return x * lax.rsqrt(ms + RMS_EPS) * gain


def _gelu_tanh(y):
    c = math.sqrt(2.0 / math.pi)
    return 0.5 * y * (1.0 + jnp.tanh(c * (y + 0.044715 * (y * y * y))))


def _sigmoid(x):
    return 1.0 / (1.0 + jnp.exp(-x))


def _const_spec(shape):
    nd = len(shape)
    return pl.BlockSpec(shape, lambda *_: (0,) * nd, pipeline_mode=pl.Buffered(1))


def _layer_spec(shape, layer):
    rest = (0,) * (len(shape) - 1)
    return pl.BlockSpec((None,) + tuple(shape[1:]), lambda *_: (layer,) + rest,
                        pipeline_mode=pl.Buffered(1))


def _s5_kernel(x_ref, gain_ref, bw_ref, avec_ref, cw_ref, d_ref, wglu_ref, init_ref,
               out_ref, fin_ref, state_ref, hs_ref, bu_ref, *, chunk):
    half = S5_SLAB_STATE
    lane_slabs = D_MODEL // LANES

    @pl.when(pl.program_id(0) == 0)
    def _():
        state_ref[...] = init_ref[...]

    for b in range(BATCH):
        for j in range(lane_slabs):
            hs_ref[j, pl.ds(b, chunk, stride=BATCH), :] = x_ref[b, :, j * LANES:(j + 1) * LANES]
    h = jnp.concatenate([hs_ref[j] for j in range(lane_slabs)], axis=1)
    u = _rmsnorm(h, gain_ref[...])
    ub = u.astype(BF16)
    zs = []
    for s in range(S5_SLABS):
        chans = slice(s * MXU_DIM, (s + 1) * MXU_DIM)
        bu_ref[...] = jnp.dot(ub[:, chans], bw_ref[s], preferred_element_type=F32)
        a_re = avec_ref[s, :, :half]
        a_im = avec_ref[s, :, half:]

        def step(l, carry, a_re=a_re, a_im=a_im):
            x_re, x_im = carry
            row = pl.multiple_of(l * SUBLANES, SUBLANES)
            b_re = bu_ref[pl.ds(row, SUBLANES), :half]
            b_im = bu_ref[pl.ds(row, SUBLANES), half:]
            n_re = a_re * x_re - a_im * x_im + b_re
            n_im = a_re * x_im + a_im * x_re + b_im
            bu_ref[pl.ds(row, SUBLANES), :half] = n_re
            bu_ref[pl.ds(row, SUBLANES), half:] = n_im
            return n_re, n_im

        x_re, x_im = lax.fori_loop(0, chunk, step,
                                   (state_ref[s, :, :half], state_ref[s, :, half:]), unroll=True)
        state_ref[s, :, :half] = x_re
        state_ref[s, :, half:] = x_im
        y = jnp.dot(bu_ref[...].astype(BF16), cw_ref[s], preferred_element_type=F32)
        zs.append(_gelu_tanh(y + d_ref[:, chans] * u[:, chans]).astype(BF16))

    vg = jnp.dot(jnp.concatenate(zs, axis=1), wglu_ref[...], preferred_element_type=F32)
    res = h + vg[:, :D_MODEL] * _sigmoid(vg[:, D_MODEL:])
    for j in range(lane_slabs):
        hs_ref[j] = res[:, j * LANES:(j + 1) * LANES]
    for b in range(BATCH):
        for j in range(lane_slabs):
            out_ref[b, :, j * LANES:(j + 1) * LANES] = hs_ref[j, pl.ds(b, chunk, stride=BATCH), :]

    @pl.when(pl.program_id(0) == pl.num_programs(0) - 1)
    def _():
        fin_ref[...] = state_ref[...]


def _s5_layer(x, gain, bw, avec, cw, d_skip, wglu, init_state, *, chunk):
    length = x.shape[1]
    m = chunk * BATCH
    assert length % chunk == 0
    state_shape = (S5_SLABS, BATCH, 2 * S5_SLAB_STATE)
    io_spec = pl.BlockSpec((BATCH, chunk, D_MODEL), lambda i: (0, i, 0))
    return pl.pallas_call(
        functools.partial(_s5_kernel, chunk=chunk),
        grid=(length // chunk,),
        in_specs=[
            io_spec,
            _const_spec((1, D_MODEL)),
            _const_spec(bw.shape),
            _const_spec(avec.shape),
            _const_spec(cw.shape),
            _const_spec((1, D_MODEL)),
            _const_spec(wglu.shape),
            _const_spec(state_shape),
        ],
        out_specs=[
            io_spec,
            pl.BlockSpec(state_shape, lambda i: (0, 0, 0)),
        ],
        out_shape=[
            jax.ShapeDtypeStruct(x.shape, F32),
            jax.ShapeDtypeStruct(state_shape, F32),
        ],
        scratch_shapes=[
            pltpu.VMEM(state_shape, F32),
            pltpu.VMEM((D_MODEL // LANES, m, LANES), F32),
            pltpu.VMEM((m, 2 * S5_SLAB_STATE), F32),
        ],
        compiler_params=pltpu.CompilerParams(
            dimension_semantics=("arbitrary",), vmem_limit_bytes=VMEM_LIMIT_BYTES),
        name="s5_layer",
    )(x, gain, bw, avec, cw, d_skip, wglu, init_state)


def _swiglu(hn, win_ref, wo_ref):
    acc = None
    for f0, f1 in FFN_F_CHUNKS:
        g = jnp.dot(hn, win_ref[:, f0:f1], preferred_element_type=F32)
        u = jnp.dot(hn, win_ref[:, D_FF + f0:D_FF + f1], preferred_element_type=F32)
        a = (g * _sigmoid(g) * u).astype(BF16)
        part = jnp.dot(a, wo_ref[f0:f1, :], preferred_element_type=F32)
        acc = part if acc is None else acc + part
    return acc


def _ffn_kernel(h_ref, gain_ref, win_ref, wo_ref, out_ref):
    h = h_ref[...]
    hn = _rmsnorm(h, gain_ref[...]).astype(BF16)
    out_ref[...] = h + _swiglu(hn, win_ref, wo_ref)


def _ffn_layer(h, gain, w_in, w_out, *, layer, rows_per_step):
    rows = h.shape[0]
    assert rows % rows_per_step == 0
    return pl.pallas_call(
        _ffn_kernel,
        grid=(rows // rows_per_step,),
        in_specs=[
            pl.BlockSpec((rows_per_step, D_MODEL), lambda i: (i, 0)),
            _const_spec((1, D_MODEL)),
            _layer_spec(w_in.shape, layer),
            _layer_spec(w_out.shape, layer),
        ],
        out_specs=pl.BlockSpec((rows_per_step, D_MODEL), lambda i: (i, 0)),
        out_shape=jax.ShapeDtypeStruct((rows, D_MODEL), F32),
        compiler_params=pltpu.CompilerParams(
            dimension_semantics=("parallel",), vmem_limit_bytes=VMEM_LIMIT_BYTES),
        name="ffn_layer",
    )(h, gain, w_in, w_out)


def _proj_kernel(h_ref, gq_ref, gkv_ref, wq_ref, wkv_ref, q_ref, k_ref, v_ref):
    h = h_ref[...]
    y = h * lax.rsqrt(jnp.mean(h * h, axis=-1, keepdims=True) + RMS_EPS)
    hq = (y * gq_ref[...]).astype(BF16)
    hkv = (y * gkv_ref[...]).astype(BF16)
    scale = LOG2E / math.sqrt(HEAD_DIM)
    q_ref[...] = (jnp.dot(hq, wq_ref[...], preferred_element_type=F32) * scale).astype(BF16)
    hd = q_ref.shape[1]
    k_ref[...] = jnp.dot(hkv, wkv_ref[:, :hd], preferred_element_type=F32).astype(BF16)
    v_ref[...] = jnp.dot(hkv, wkv_ref[:, hd:], preferred_element_type=F32).astype(BF16)


def _proj_layer(h, gq, gkv, wq, wkv, *, rows_per_step):
    rows = h.shape[0]
    assert rows % rows_per_step == 0
    hd = N_HEADS * HEAD_DIM
    row_spec = pl.BlockSpec((rows_per_step, hd), lambda i: (i, 0))
    return pl.pallas_call(
        _proj_kernel,
        grid=(rows // rows_per_step,),
        in_specs=[
            pl.BlockSpec((rows_per_step, D_MODEL), lambda i: (i, 0)),
            _const_spec((1, D_MODEL)),
            _const_spec((1, D_MODEL)),
            _const_spec(wq.shape),
            _const_spec(wkv.shape),
        ],
        out_specs=[row_spec, row_spec, row_spec],
        out_shape=[jax.ShapeDtypeStruct((rows, hd), BF16)] * 3,
        compiler_params=pltpu.CompilerParams(
            dimension_semantics=("parallel",), vmem_limit_bytes=VMEM_LIMIT_BYTES),
        name="qkv_proj",
    )(h, gq, gkv, wq, wkv)


def _attn_kernel(q_ref, k_ref, v_ref, km_ref, vm_ref, tri_ref, trim_ref, dbias_ref, mbias_ref,
                 o_ref, acc_ref, rem_ref):
    head_of_lane = lax.broadcasted_iota(jnp.int32, (1, ATT_LANES), 1) // HEAD_DIM

    def block(t, q_st, kb, vb, tri_neg, bias):
        nk = kb.shape[0]
        z = lax.dot_general(q_st, kb, (((1,), (1,)), ((), ())),
                            preferred_element_type=F32)
        if bias is not None:
            z = z + bias
        sp = jnp.where(z > SOFTPLUS_CUT, z, jnp.log2(1.0 + jnp.exp2(z)))
        suffix = jnp.dot(sp.astype(BF16), tri_neg, preferred_element_type=F32)
        rem = rem_ref[t]
        rem_b = jnp.concatenate([rem] * (nk // LANES), axis=1)
        w = jnp.exp2((z - sp) + suffix + rem_b)
        acc_ref[t] += jnp.dot(w.astype(BF16), vb, preferred_element_type=F32)
        rem_ref[t] = rem - jnp.sum(sp, axis=1, keepdims=True)

    def kv_block(j):
        k0 = pl.multiple_of(j * ATT_K, ATT_K)
        return k_ref[pl.ds(k0, ATT_K), :], v_ref[pl.ds(k0, ATT_K), :]

    def stacked_q(qi):
        q = q_ref[pl.ds(pl.multiple_of(qi * ATT_Q, ATT_Q), ATT_Q), :]
        zero = jnp.zeros_like(q)
        return jnp.concatenate([jnp.where(head_of_lane == hd, q, zero) for hd in range(ATT_HEADS)],
                               axis=0)

    def live_flag(t):
        return (jnp.max(rem_ref[t]) >= REM_FLOOR).astype(jnp.int32)

    def finish(t, qi, q_st, live0, j0):
        def earlier(c):
            j, _ = c
            block(t, q_st, *kv_block(j), tri_ref[...], None)
            return j - 1, live_flag(t)

        _, live = lax.while_loop(lambda c: jnp.logical_and(c[0] >= 0, c[1] > 0), earlier,
                                 (j0, live0))

        @pl.when(live > 0)
        def _():
            block(t, q_st, km_ref[...], vm_ref[...], trim_ref[...], mbias_ref[...])

        o_t = acc_ref[t, :ATT_Q]
        for hd in range(1, ATT_HEADS):
            o_t = jnp.where(head_of_lane == hd, acc_ref[t, hd * ATT_Q:(hd + 1) * ATT_Q], o_t)
        o_ref[pl.ds(pl.multiple_of(qi * ATT_Q, ATT_Q), ATT_Q), :] = o_t.astype(o_ref.dtype)

    def tile_group(q_first, first):
        tiles = [q_first + t for t in range(ATT_GROUP)]
        qs = [stacked_q(qi) for qi in tiles]
        kvs = [kv_block(qi) for qi in tiles]
        acc_ref[...] = jnp.zeros_like(acc_ref)
        rem_ref[...] = jnp.zeros_like(rem_ref)
        for t in range(ATT_GROUP):
            block(t, qs[t], *kvs[t], tri_ref[...], dbias_ref[...])
            if t > 0:
                block(t, qs[t], *kvs[t - 1], tri_ref[...], None)
            elif not first:
                block(t, qs[t], *kv_block(q_first - 1), tri_ref[...], None)
        lives = [jnp.int32(1) if (first and t == 0) else live_flag(t) for t in range(ATT_GROUP)]
        for t in range(ATT_GROUP):
            finish(t, tiles[t], qs[t], lives[t], tiles[t] - 2)

    tile_group(0, True)
    groups = SEQ // (ATT_GROUP * ATT_Q)
    if groups > 1:
        def later_group(i, carry):
            tile_group(ATT_GROUP * i, False)
            return carry

        lax.fori_loop(1, groups, later_group, 0)


def _attention(q, k, v, k_meta, v_meta):
    pairs = N_HEADS // ATT_HEADS
    seq_spec = pl.BlockSpec((None, SEQ, ATT_LANES), lambda b, p: (b, 0, p))
    meta_spec = pl.BlockSpec((META_PAD, ATT_LANES), lambda b, p: (0, p))
    tri = _suffix_matrix(ATT_K)
    tri_meta = _suffix_matrix(META_PAD)
    row = lax.broadcasted_iota(jnp.int32, (ATT_Q, ATT_K), 0)
    col = lax.broadcasted_iota(jnp.int32, (ATT_Q, ATT_K), 1)
    diag_bias = jnp.where(col < row, 0.0, MASKED_LOGIT).astype(F32)
    diag_bias = jnp.concatenate([diag_bias] * ATT_HEADS, axis=0)
    mcol = lax.broadcasted_iota(jnp.int32, (1, META_PAD), 1)
    meta_bias = jnp.where(mcol < N_META, 0.0, MASKED_LOGIT).astype(F32)
    return pl.pallas_call(
        _attn_kernel,
        grid=(BATCH, pairs),
        in_specs=[
            seq_spec, seq_spec, seq_spec, meta_spec, meta_spec,
            _const_spec(tri.shape), _const_spec(tri_meta.shape),
            _const_spec(diag_bias.shape), _const_spec(meta_bias.shape),
        ],
        out_specs=seq_spec,
        out_shape=jax.ShapeDtypeStruct(q.shape, BF16),
        scratch_shapes=[
            pltpu.VMEM((ATT_GROUP, ATT_HEADS * ATT_Q, ATT_LANES), F32),
            pltpu.VMEM((ATT_GROUP, ATT_HEADS * ATT_Q, LANES), F32),
        ],
        compiler_params=pltpu.CompilerParams(
            dimension_semantics=("parallel", "parallel"), vmem_limit_bytes=VMEM_LIMIT_BYTES),
        name="stickbreak_attn",
    )(q, k, v, k_meta, v_meta, tri, tri_meta, diag_bias, meta_bias)


def _out_kernel(h_ref, o_ref, wo_att_ref, gain_ref, win_ref, wo_ref, gfin_ref, out_ref):
    h = h_ref[...] + jnp.dot(o_ref[...], wo_att_ref[...], preferred_element_type=F32)
    hn = _rmsnorm(h, gain_ref[...]).astype(BF16)
    h = h + _swiglu(hn, win_ref, wo_ref)
    out_ref[...] = _rmsnorm(h, gfin_ref[...])


def _out_layer(h, o, wo_att, gain, w_in, w_out, gfin, *, layer, rows_per_step):
    rows = h.shape[0]
    assert rows % rows_per_step == 0
    row_spec = pl.BlockSpec((rows_per_step, D_MODEL), lambda i: (i, 0))
    return pl.pallas_call(
        _out_kernel,
        grid=(rows // rows_per_step,),
        in_specs=[
            row_spec,
            pl.BlockSpec((rows_per_step, o.shape[1]), lambda i: (i, 0)),
            _const_spec(wo_att.shape),
            _const_spec((1, D_MODEL)),
            _layer_spec(w_in.shape, layer),
            _layer_spec(w_out.shape, layer),
            _const_spec((1, D_MODEL)),
        ],
        out_specs=row_spec,
        out_shape=jax.ShapeDtypeStruct((rows, D_MODEL), F32),
        compiler_params=pltpu.CompilerParams(
            dimension_semantics=("parallel",), vmem_limit_bytes=VMEM_LIMIT_BYTES),
        name="oproj_ffn_final",
    )(h, o, wo_att, gain, w_in, w_out, gfin)


def _s5_weights(a_re, a_im, log_dt, b_re, b_im, c_re, c_im):
    dt = jnp.exp(log_dt)[:, None]
    mag = jnp.exp(dt * a_re)
    ang = dt * a_im
    abar_re = mag * jnp.cos(ang)
    abar_im = mag * jnp.sin(ang)
    den = a_re * a_re + a_im * a_im
    coef_re = ((abar_re - 1.0) * a_re + abar_im * a_im) / den
    coef_im = (abar_im * a_re - (abar_re - 1.0) * a_im) / den
    bbar_re = coef_re[..., None] * b_re - coef_im[..., None] * b_im
    bbar_im = coef_re[..., None] * b_im + coef_im[..., None] * b_re
    same_group = jnp.eye(S5_SLAB_GROUPS, dtype=bool)[None, :, None, :, None]
    slab = (S5_SLABS, S5_SLAB_GROUPS)

    def block_diag(w):
        full = jnp.where(same_group, w[:, :, :, None, :], 0.0)
        return full.reshape(S5_SLABS, S5_SLAB_GROUPS * w.shape[2], S5_SLAB_GROUPS * w.shape[3])

    def b_slab(bb):
        return block_diag(jnp.swapaxes(bb.reshape(slab + (S5_STATE, S5_GROUP)), 2, 3))

    def c_slab(cc):
        return block_diag(jnp.swapaxes(cc.reshape(slab + (S5_GROUP, S5_STATE)), 2, 3))

    bw = jnp.concatenate([b_slab(bbar_re), b_slab(bbar_im)], axis=2).astype(BF16)
    cw = jnp.concatenate([c_slab(c_re), -c_slab(c_im)], axis=1).astype(BF16)
    avec = jnp.concatenate([abar_re.reshape(S5_SLABS, S5_SLAB_STATE),
                            abar_im.reshape(S5_SLABS, S5_SLAB_STATE)], axis=1)
    avec = jnp.broadcast_to(avec[:, None, :], (S5_SLABS, BATCH, 2 * S5_SLAB_STATE))
    return bw, avec, cw


def _suffix_matrix(n):
    j = lax.broadcasted_iota(jnp.int32, (n, n), 0)
    s = lax.broadcasted_iota(jnp.int32, (n, n), 1)
    return jnp.where(j > s, -1.0, 0.0).astype(BF16)


def kernel(x, meta_tokens, norm_mix, norm_ffn, s5_a_re, s5_a_im, s5_log_dt, s5_b_re, s5_b_im,
           s5_c_re, s5_c_im, s5_d, s5_w_glu, norm_kv, w_kv, w_q, w_o, w_ffn_in, w_ffn_out,
           norm_final):
    hd = N_HEADS * HEAD_DIM
    row = lambda g: g.reshape(1, D_MODEL)
    bw, avec, cw = _s5_weights(s5_a_re[0], s5_a_im[0], s5_log_dt[0], s5_b_re[0], s5_b_im[0],
                               s5_c_re[0], s5_c_im[0])
    wglu = s5_w_glu[0].astype(BF16)
    w_in = w_ffn_in.astype(BF16)
    w_out = w_ffn_out.astype(BF16)
    wq = w_q[0].astype(BF16)
    wkv = w_kv.astype(BF16)
    wo_att = w_o[0].astype(BF16)

    s5 = functools.partial(_s5_layer, gain=row(norm_mix[0]), bw=bw, avec=avec, cw=cw,
                           d_skip=row(s5_d[0]), wglu=wglu)
    x_meta = jnp.broadcast_to(meta_tokens.astype(x.dtype)[None], (BATCH, N_META, D_MODEL))
    zero_state = jnp.zeros((S5_SLABS, BATCH, 2 * S5_SLAB_STATE), F32)
    h_meta, meta_state = s5(x_meta, init_state=zero_state, chunk=N_META)
    h_real, _ = s5(x, init_state=meta_state, chunk=S5_CHUNK)
    h_meta = h_meta[0]
    h_real = h_real.reshape(BATCH * SEQ, D_MODEL)

    ffn0 = functools.partial(_ffn_layer, gain=row(norm_ffn[0]), w_in=w_in, w_out=w_out, layer=0)
    h_meta = ffn0(h_meta, rows_per_step=N_META)
    h_real = ffn0(h_real, rows_per_step=FFN_ROWS)

    proj = functools.partial(_proj_layer, gq=row(norm_mix[1]), gkv=row(norm_kv),
                             wq=wq, wkv=wkv)
    _, k_meta, v_meta = proj(h_meta, rows_per_step=N_META)
    q, k, v = proj(h_real, rows_per_step=PROJ_ROWS)

    pad = ((0, META_PAD - N_META), (0, 0))
    seq = lambda a: a.reshape(BATCH, SEQ, hd)
    o = _attention(seq(q), seq(k), seq(v), jnp.pad(k_meta, pad), jnp.pad(v_meta, pad))

    out = _out_layer(h_real, o.reshape(BATCH * SEQ, hd), wo_att, row(norm_ffn[1]),
                     w_in, w_out, row(norm_final), layer=1, rows_per_step=FFN_ROWS)
    return out.reshape(BATCH, SEQ, D_MODEL)
```

```python
import functools
import math

import jax
import jax.numpy as jnp
from jax import lax
from jax.experimental import pallas as pl
from jax.experimental.pallas import tpu as pltpu

D_MODEL = 1024
BATCH = 8
SEQ = 4096
N_META = 16
S5_GROUP = 16
S5_GROUPS = 64
S5_STATE = 64
N_HEADS = 16
HEAD_DIM = 64
D_FF = 2816
RMS_EPS = 1e-6

LANES = 128
SUBLANES = 8
MXU_DIM = 256
VMEM_LIMIT_BYTES = 56 * 1024 * 1024

S5_SLAB_GROUPS = MXU_DIM // S5_GROUP
S5_SLABS = S5_GROUPS // S5_SLAB_GROUPS
S5_SLAB_STATE = S5_SLAB_GROUPS * S5_STATE
S5_CHUNK = 64

FFN_ROWS = 1024
FFN_F_CHUNKS = ((0, 1024), (1024, 2048), (2048, D_FF))
PROJ_ROWS = 1024
FFN_QKV_ROWS = 512

ATT_Q = 256
ATT_K = 256
ATT_GROUP = 16
ATT_HEADS = 2
ATT_LANES = ATT_HEADS * HEAD_DIM
META_PAD = 128
MASKED_LOGIT = -1e30
REM_FLOOR = -151.0
SOFTPLUS_CUT = 64.0
LOG2E = 1.4426950408889634

F32 = jnp.float32
BF16 = jnp.bfloat16


def _rmsnorm(x, gain):
    ms = jnp.mean(x * x, axis=-1, keepdims=True)
    return x * lax.rsqrt(ms + RMS_EPS) * gain


def _gelu_tanh(y):
    c = math.sqrt(2.0 / math.pi)
    return 0.5 * y * (1.0 + jnp.tanh(c * (y + 0.044715 * (y * y * y))))


def _sigmoid(x):
    return 1.0 / (1.0 + jnp.exp(-x))


def _const_spec(shape):
    nd = len(shape)
    return pl.BlockSpec(shape, lambda *_: (0,) * nd, pipeline_mode=pl.Buffered(1))


def _layer_spec(shape, layer):
    rest = (0,) * (len(shape) - 1)
    return pl.BlockSpec((None,) + tuple(shape[1:]), lambda *_: (layer,) + rest,
                        pipeline_mode=pl.Buffered(1))


def _s5_kernel(x_ref, gain_ref, bw_ref, avec_ref, cw_ref, d_ref, wglu_ref, init_ref,
               out_ref, fin_ref, state_ref, hs_ref, bu_ref, *, chunk):
    half = S5_SLAB_STATE
    lane_slabs = D_MODEL // LANES

    @pl.when(pl.program_id(0) == 0)
    def _():
        state_ref[...] = init_ref[...]

    for b in range(BATCH):
        for j in range(lane_slabs):
            hs_ref[j, pl.ds(b, chunk, stride=BATCH), :] = x_ref[b, :, j * LANES:(j + 1) * LANES]
    h = jnp.concatenate([hs_ref[j] for j in range(lane_slabs)], axis=1)
    u = _rmsnorm(h, gain_ref[...])
    ub = u.astype(BF16)
    zs = []
    for s in range(S5_SLABS):
        chans = slice(s * MXU_DIM, (s + 1) * MXU_DIM)
        bu_ref[...] = jnp.dot(ub[:, chans], bw_ref[s], preferred_element_type=F32)
        a_re = avec_ref[s, :, :half]
        a_im = avec_ref[s, :, half:]

        def step(l, carry, a_re=a_re, a_im=a_im):
            x_re, x_im = carry
            row = pl.multiple_of(l * SUBLANES, SUBLANES)
            b_re = bu_ref[pl.ds(row, SUBLANES), :half]
            b_im = bu_ref[pl.ds(row, SUBLANES), half:]
            n_re = a_re * x_re - a_im * x_im + b_re
            n_im = a_re * x_im + a_im * x_re + b_im
            bu_ref[pl.ds(row, SUBLANES), :half] = n_re
            bu_ref[pl.ds(row, SUBLANES), half:] = n_im
            return n_re, n_im

        x_re, x_im = lax.fori_loop(0, chunk, step,
                                   (state_ref[s, :, :half], state_ref[s, :, half:]), unroll=True)
        state_ref[s, :, :half] = x_re
        state_ref[s, :, half:] = x_im
        y = jnp.dot(bu_ref[...].astype(BF16), cw_ref[s], preferred_element_type=F32)
        zs.append(_gelu_tanh(y + d_ref[:, chans] * u[:, chans]).astype(BF16))

    vg = jnp.dot(jnp.concatenate(zs, axis=1), wglu_ref[...], preferred_element_type=F32)
    res = h + vg[:, :D_MODEL] * _sigmoid(vg[:, D_MODEL:])
    for j in range(lane_slabs):
        hs_ref[j] = res[:, j * LANES:(j + 1) * LANES]
    for b in range(BATCH):
        for j in range(lane_slabs):
            out_ref[b, :, j * LANES:(j + 1) * LANES] = hs_ref[j, pl.ds(b, chunk, stride=BATCH), :]

    @pl.when(pl.program_id(0) == pl.num_programs(0) - 1)
    def _():
        fin_ref[...] = state_ref[...]


def _s5_layer(x, gain, bw, avec, cw, d_skip, wglu, init_state, *, chunk):
    length = x.shape[1]
    m = chunk * BATCH
    assert length % chunk == 0
    state_shape = (S5_SLABS, BATCH, 2 * S5_SLAB_STATE)
    io_spec = pl.BlockSpec((BATCH, chunk, D_MODEL), lambda i: (0, i, 0))
    return pl.pallas_call(
        functools.partial(_s5_kernel, chunk=chunk),
        grid=(length // chunk,),
        in_specs=[
            io_spec,
            _const_spec((1, D_MODEL)),
            _const_spec(bw.shape),
            _const_spec(avec.shape),
            _const_spec(cw.shape),
            _const_spec((1, D_MODEL)),
            _const_spec(wglu.shape),
            _const_spec(state_shape),
        ],
        out_specs=[
            io_spec,
            pl.BlockSpec(state_shape, lambda i: (0, 0, 0)),
        ],
        out_shape=[
            jax.ShapeDtypeStruct(x.shape, F32),
            jax.ShapeDtypeStruct(state_shape, F32),
        ],
        scratch_shapes=[
            pltpu.VMEM(state_shape, F32),
            pltpu.VMEM((D_MODEL // LANES, m, LANES), F32),
            pltpu.VMEM((m, 2 * S5_SLAB_STATE), F32),
        ],
        compiler_params=pltpu.CompilerParams(
            dimension_semantics=("arbitrary",), vmem_limit_bytes=VMEM_LIMIT_BYTES),
        name="s5_layer",
    )(x, gain, bw, avec, cw, d_skip, wglu, init_state)


def _swiglu(hn, win_ref, wo_ref):
    acc = None
    for f0, f1 in FFN_F_CHUNKS:
        g = jnp.dot(hn, win_ref[:, f0:f1], preferred_element_type=F32)
        u = jnp.dot(hn, win_ref[:, D_FF + f0:D_FF + f1], preferred_element_type=F32)
        a = (g * _sigmoid(g) * u).astype(BF16)
        part = jnp.dot(a, wo_ref[f0:f1, :], preferred_element_type=F32)
        acc = part if acc is None else acc + part
    return acc


def _ffn_kernel(h_ref, gain_ref, win_ref, wo_ref, out_ref):
    h = h_ref[...]
    hn = _rmsnorm(h, gain_ref[...]).astype(BF16)
    out_ref[...] = h + _swiglu(hn, win_ref, wo_ref)


def _ffn_layer(h, gain, w_in, w_out, *, layer, rows_per_step):
    rows = h.shape[0]
    assert rows % rows_per_step == 0
    return pl.pallas_call(
        _ffn_kernel,
        grid=(rows // rows_per_step,),
        in_specs=[
            pl.BlockSpec((rows_per_step, D_MODEL), lambda i: (i, 0)),
            _const_spec((1, D_MODEL)),
            _layer_spec(w_in.shape, layer),
            _layer_spec(w_out.shape, layer),
        ],
        out_specs=pl.BlockSpec((rows_per_step, D_MODEL), lambda i: (i, 0)),
        out_shape=jax.ShapeDtypeStruct((rows, D_MODEL), F32),
        compiler_params=pltpu.CompilerParams(
            dimension_semantics=("parallel",), vmem_limit_bytes=VMEM_LIMIT_BYTES),
        name="ffn_layer",
    )(h, gain, w_in, w_out)


def _ffn_proj_kernel(h_ref, gain_ref, win_ref, wo_ref, gq_ref, gkv_ref, wq_ref, wkv_ref,
                     out_ref, q_ref, k_ref, v_ref):
    h = h_ref[...]
    hn = _rmsnorm(h, gain_ref[...]).astype(BF16)
    h = h + _swiglu(hn, win_ref, wo_ref)
    out_ref[...] = h
    y = h * lax.rsqrt(jnp.mean(h * h, axis=-1, keepdims=True) + RMS_EPS)
    hq = (y * gq_ref[...]).astype(BF16)
    hkv = (y * gkv_ref[...]).astype(BF16)
    scale = LOG2E / math.sqrt(HEAD_DIM)
    q_ref[...] = (jnp.dot(hq, wq_ref[...], preferred_element_type=F32) * scale).astype(BF16)
    hd = q_ref.shape[1]
    k_ref[...] = jnp.dot(hkv, wkv_ref[:, :hd], preferred_element_type=F32).astype(BF16)
    v_ref[...] = jnp.dot(hkv, wkv_ref[:, hd:], preferred_element_type=F32).astype(BF16)


def _ffn_proj_layer(h, gain, w_in, w_out, gq, gkv, wq, wkv, *, layer, rows_per_step):
    rows = h.shape[0]
    assert rows % rows_per_step == 0
    hd = N_HEADS * HEAD_DIM
    row_spec = pl.BlockSpec((rows_per_step, D_MODEL), lambda i: (i, 0))
    head_spec = pl.BlockSpec((rows_per_step, hd), lambda i: (i, 0))
    return pl.pallas_call(
        _ffn_proj_kernel,
        grid=(rows // rows_per_step,),
        in_specs=[
            row_spec,
            _const_spec((1, D_MODEL)),
            _layer_spec(w_in.shape, layer),
            _layer_spec(w_out.shape, layer),
            _const_spec((1, D_MODEL)),
            _const_spec((1, D_MODEL)),
            _const_spec(wq.shape),
            _const_spec(wkv.shape),
        ],
        out_specs=[row_spec, head_spec, head_spec, head_spec],
        out_shape=[jax.ShapeDtypeStruct((rows, D_MODEL), F32)]
                  + [jax.ShapeDtypeStruct((rows, hd), BF16)] * 3,
        compiler_params=pltpu.CompilerParams(
            dimension_semantics=("parallel",), vmem_limit_bytes=VMEM_LIMIT_BYTES),
        name="ffn_qkv",
    )(h, gain, w_in, w_out, gq, gkv, wq, wkv)


def _proj_kernel(h_ref, gq_ref, gkv_ref, wq_ref, wkv_ref, q_ref, k_ref, v_ref):
    h = h_ref[...]
    y = h * lax.rsqrt(jnp.mean(h * h, axis=-1, keepdims=True) + RMS_EPS)
    hq = (y * gq_ref[...]).astype(BF16)
    hkv = (y * gkv_ref[...]).astype(BF16)
    scale = LOG2E / math.sqrt(HEAD_DIM)
    q_ref[...] = (jnp.dot(hq, wq_ref[...], preferred_element_type=F32) * scale).astype(BF16)
    hd = q_ref.shape[1]
    k_ref[...] = jnp.dot(hkv, wkv_ref[:, :hd], preferred_element_type=F32).astype(BF16)
    v_ref[...] = jnp.dot(hkv, wkv_ref[:, hd:], preferred_element_type=F32).astype(BF16)


def _proj_layer(h, gq, gkv, wq, wkv, *, rows_per_step):
    rows = h.shape[0]
    assert rows % rows_per_step == 0
    hd = N_HEADS * HEAD_DIM
    row_spec = pl.BlockSpec((rows_per_step, hd), lambda i: (i, 0))
    return pl.pallas_call(
        _proj_kernel,
        grid=(rows // rows_per_step,),
        in_specs=[
            pl.BlockSpec((rows_per_step, D_MODEL), lambda i: (i, 0)),
            _const_spec((1, D_MODEL)),
            _const_spec((1, D_MODEL)),
            _const_spec(wq.shape),
            _const_spec(wkv.shape),
        ],
        out_specs=[row_spec, row_spec, row_spec],
        out_shape=[jax.ShapeDtypeStruct((rows, hd), BF16)] * 3,
        compiler_params=pltpu.CompilerParams(
            dimension_semantics=("parallel",), vmem_limit_bytes=VMEM_LIMIT_BYTES),
        name="qkv_proj",
    )(h, gq, gkv, wq, wkv)


def _attn_kernel(q_ref, k_ref, v_ref, km_ref, vm_ref, tri_ref, trim_ref, dbias_ref, mbias_ref,
                 o_ref, acc_ref, rem_ref):
    head_of_lane = lax.broadcasted_iota(jnp.int32, (1, ATT_LANES), 1) // HEAD_DIM

    def block(t, q_st, kb, vb, tri_neg, bias):
        nk = kb.shape[0]
        z = lax.dot_general(q_st, kb, (((1,), (1,)), ((), ())),
                            preferred_element_type=F32)
        if bias is not None:
            z = z + bias
        sp = jnp.where(z > SOFTPLUS_CUT, z, jnp.log2(1.0 + jnp.exp2(z)))
        suffix = jnp.dot(sp.astype(BF16), tri_neg, preferred_element_type=F32)
        rem = rem_ref[t]
        rem_b = jnp.concatenate([rem] * (nk // LANES), axis=1)
        w = jnp.exp2((z - sp) + suffix + rem_b)
        acc_ref[t] += jnp.dot(w.astype(BF16), vb, preferred_element_type=F32)
        rem_ref[t] = rem - jnp.sum(sp, axis=1, keepdims=True)

    def kv_block(j):
        k0 = pl.multiple_of(j * ATT_K, ATT_K)
        return k_ref[pl.ds(k0, ATT_K), :], v_ref[pl.ds(k0, ATT_K), :]

    def stacked_q(qi):
        q = q_ref[pl.ds(pl.multiple_of(qi * ATT_Q, ATT_Q), ATT_Q), :]
        zero = jnp.zeros_like(q)
        return jnp.concatenate([jnp.where(head_of_lane == hd, q, zero) for hd in range(ATT_HEADS)],
                               axis=0)

    def live_flag(t):
        return (jnp.max(rem_ref[t]) >= REM_FLOOR).astype(jnp.int32)

    def finish(t, qi, q_st, live0, j0):
        def earlier(c):
            j, _ = c
            block(t, q_st, *kv_block(j), tri_ref[...], None)
            return j - 1, live_flag(t)

        _, live = lax.while_loop(lambda c: jnp.logical_and(c[0] >= 0, c[1] > 0), earlier,
                                 (j0, live0))

        @pl.when(live > 0)
        def _():
            block(t, q_st, km_ref[...], vm_ref[...], trim_ref[...], mbias_ref[...])

        o_t = acc_ref[t, :ATT_Q]
        for hd in range(1, ATT_HEADS):
            o_t = jnp.where(head_of_lane == hd, acc_ref[t, hd * ATT_Q:(hd + 1) * ATT_Q], o_t)
        o_ref[pl.ds(pl.multiple_of(qi * ATT_Q, ATT_Q), ATT_Q), :] = o_t.astype(o_ref.dtype)

    def tile_group(q_first, first):
        tiles = [q_first + t for t in range(ATT_GROUP)]
        qs = [stacked_q(qi) for qi in tiles]
        kvs = [kv_block(qi) for qi in tiles]
        acc_ref[...] = jnp.zeros_like(acc_ref)
        rem_ref[...] = jnp.zeros_like(rem_ref)
        for t in range(ATT_GROUP):
            block(t, qs[t], *kvs[t], tri_ref[...], dbias_ref[...])
            if t > 0:
                block(t, qs[t], *kvs[t - 1], tri_ref[...], None)
            elif not first:
                block(t, qs[t], *kv_block(q_first - 1), tri_ref[...], None)
        lives = [jnp.int32(1) if (first and t == 0) else live_flag(t) for t in range(ATT_GROUP)]
        for t in range(ATT_GROUP):
            finish(t, tiles[t], qs[t], lives[t], tiles[t] - 2)

    tile_group(0, True)
    groups = SEQ // (ATT_GROUP * ATT_Q)
    if groups > 1:
        def later_group(i, carry):
            tile_group(ATT_GROUP * i, False)
            return carry

        lax.fori_loop(1, groups, later_group, 0)


def _attention(q, k, v, k_meta, v_meta):
    pairs = N_HEADS // ATT_HEADS
    seq_spec = pl.BlockSpec((None, SEQ, ATT_LANES), lambda b, p: (b, 0, p))
    meta_spec = pl.BlockSpec((META_PAD, ATT_LANES), lambda b, p: (0, p))
    tri = _suffix_matrix(ATT_K)
    tri_meta = _suffix_matrix(META_PAD)
    row = lax.broadcasted_iota(jnp.int32, (ATT_Q, ATT_K), 0)
    col = lax.broadcasted_iota(jnp.int32, (ATT_Q, ATT_K), 1)
    diag_bias = jnp.where(col < row, 0.0, MASKED_LOGIT).astype(F32)
    diag_bias = jnp.concatenate([diag_bias] * ATT_HEADS, axis=0)
    mcol = lax.broadcasted_iota(jnp.int32, (1, META_PAD), 1)
    meta_bias = jnp.where(mcol < N_META, 0.0, MASKED_LOGIT).astype(F32)
    return pl.pallas_call(
        _attn_kernel,
        grid=(BATCH, pairs),
        in_specs=[
            seq_spec, seq_spec, seq_spec, meta_spec, meta_spec,
            _const_spec(tri.shape), _const_spec(tri_meta.shape),
            _const_spec(diag_bias.shape), _const_spec(meta_bias.shape),
        ],
        out_specs=seq_spec,
        out_shape=jax.ShapeDtypeStruct(q.shape, BF16),
        scratch_shapes=[
            pltpu.VMEM((ATT_GROUP, ATT_HEADS * ATT_Q, ATT_LANES), F32),
            pltpu.VMEM((ATT_GROUP, ATT_HEADS * ATT_Q, LANES), F32),
        ],
        compiler_params=pltpu.CompilerParams(
            dimension_semantics=("parallel", "parallel"), vmem_limit_bytes=VMEM_LIMIT_BYTES),
        name="stickbreak_attn",
    )(q, k, v, k_meta, v_meta, tri, tri_meta, diag_bias, meta_bias)


def _out_kernel(h_ref, o_ref, wo_att_ref, gain_ref, win_ref, wo_ref, gfin_ref, out_ref):
    h = h_ref[...] + jnp.dot(o_ref[...], wo_att_ref[...], preferred_element_type=F32)
    hn = _rmsnorm(h, gain_ref[...]).astype(BF16)
    h = h + _swiglu(hn, win_ref, wo_ref)
    out_ref[...] = _rmsnorm(h, gfin_ref[...])


def _out_layer(h, o, wo_att, gain, w_in, w_out, gfin, *, layer, rows_per_step):
    rows = h.shape[0]
    assert rows % rows_per_step == 0
    row_spec = pl.BlockSpec((rows_per_step, D_MODEL), lambda i: (i, 0))
    return pl.pallas_call(
        _out_kernel,
        grid=(rows // rows_per_step,),
        in_specs=[
            row_spec,
            pl.BlockSpec((rows_per_step, o.shape[1]), lambda i: (i, 0)),
            _const_spec(wo_att.shape),
            _const_spec((1, D_MODEL)),
            _layer_spec(w_in.shape, layer),
            _layer_spec(w_out.shape, layer),
            _const_spec((1, D_MODEL)),
        ],
        out_specs=row_spec,
        out_shape=jax.ShapeDtypeStruct((rows, D_MODEL), F32),
        compiler_params=pltpu.CompilerParams(
            dimension_semantics=("parallel",), vmem_limit_bytes=VMEM_LIMIT_BYTES),
        name="oproj_ffn_final",
    )(h, o, wo_att, gain, w_in, w_out, gfin)


def _s5_weights(a_re, a_im, log_dt, b_re, b_im, c_re, c_im):
    dt = jnp.exp(log_dt)[:, None]
    mag = jnp.exp(dt * a_re)
    ang = dt * a_im
    abar_re = mag * jnp.cos(ang)
    abar_im = mag * jnp.sin(ang)
    den = a_re * a_re + a_im * a_im
    coef_re = ((abar_re - 1.0) * a_re + abar_im * a_im) / den
    coef_im = (abar_im * a_re - (abar_re - 1.0) * a_im) / den
    bbar_re = coef_re[..., None] * b_re - coef_im[..., None] * b_im
    bbar_im = coef_re[..., None] * b_im + coef_im[..., None] * b_re
    same_group = jnp.eye(S5_SLAB_GROUPS, dtype=bool)[None, :, None, :, None]
    slab = (S5_SLABS, S5_SLAB_GROUPS)

    def block_diag(w):
        full = jnp.where(same_group, w[:, :, :, None, :], 0.0)
        return full.reshape(S5_SLABS, S5_SLAB_GROUPS * w.shape[2], S5_SLAB_GROUPS * w.shape[3])

    def b_slab(bb):
        return block_diag(jnp.swapaxes(bb.reshape(slab + (S5_STATE, S5_GROUP)), 2, 3))

    def c_slab(cc):
        return block_diag(jnp.swapaxes(cc.reshape(slab + (S5_GROUP, S5_STATE)), 2, 3))

    bw = jnp.concatenate([b_slab(bbar_re), b_slab(bbar_im)], axis=2).astype(BF16)
    cw = jnp.concatenate([c_slab(c_re), -c_slab(c_im)], axis=1).astype(BF16)
    avec = jnp.concatenate([abar_re.reshape(S5_SLABS, S5_SLAB_STATE),
                            abar_im.reshape(S5_SLABS, S5_SLAB_STATE)], axis=1)
    avec = jnp.broadcast_to(avec[:, None, :], (S5_SLABS, BATCH, 2 * S5_SLAB_STATE))
    return bw, avec, cw


def _suffix_matrix(n):
    j = lax.broadcasted_iota(jnp.int32, (n, n), 0)
    s = lax.broadcasted_iota(jnp.int32, (n, n), 1)
    return jnp.where(j > s, -1.0, 0.0).astype(BF16)


def kernel(x, meta_tokens, norm_mix, norm_ffn, s5_a_re, s5_a_im, s5_log_dt, s5_b_re, s5_b_im,
           s5_c_re, s5_c_im, s5_d, s5_w_glu, norm_kv, w_kv, w_q, w_o, w_ffn_in, w_ffn_out,
           norm_final):
    hd = N_HEADS * HEAD_DIM
    row = lambda g: g.reshape(1, D_MODEL)
    bw, avec, cw = _s5_weights(s5_a_re[0], s5_a_im[0], s5_log_dt[0], s5_b_re[0], s5_b_im[0],
                               s5_c_re[0], s5_c_im[0])
    wglu = s5_w_glu[0].astype(BF16)
    w_in = w_ffn_in.astype(BF16)
    w_out = w_ffn_out.astype(BF16)
    wq = w_q[0].astype(BF16)
    wkv = w_kv.astype(BF16)
    wo_att = w_o[0].astype(BF16)

    s5 = functools.partial(_s5_layer, gain=row(norm_mix[0]), bw=bw, avec=avec, cw=cw,
                           d_skip=row(s5_d[0]), wglu=wglu)
    x_meta = jnp.broadcast_to(meta_tokens.astype(x.dtype)[None], (BATCH, N_META, D_MODEL))
    zero_state = jnp.zeros((S5_SLABS, BATCH, 2 * S5_SLAB_STATE), F32)
    h_meta, meta_state = s5(x_meta, init_state=zero_state, chunk=N_META)
    h_real, _ = s5(x, init_state=meta_state, chunk=S5_CHUNK)
    h_meta = h_meta[0]
    h_real = h_real.reshape(BATCH * SEQ, D_MODEL)

    ffn_qkv = functools.partial(_ffn_proj_layer, gain=row(norm_ffn[0]), w_in=w_in, w_out=w_out,
                                gq=row(norm_mix[1]), gkv=row(norm_kv), wq=wq, wkv=wkv, layer=0)
    _, _, k_meta, v_meta = ffn_qkv(h_meta, rows_per_step=N_META)
    h_real, q, k, v = ffn_qkv(h_real, rows_per_step=FFN_QKV_ROWS)

    pad = ((0, META_PAD - N_META), (0, 0))
    seq = lambda a: a.reshape(BATCH, SEQ, hd)
    o = _attention(seq(q), seq(k), seq(v), jnp.pad(k_meta, pad), jnp.pad(v_meta, pad))

    out = _out_layer(h_real, o.reshape(BATCH * SEQ, hd), wo_att, row(norm_ffn[1]),
                     w_in, w_out, row(norm_final), layer=1, rows_per_step=FFN_ROWS)
    return out.reshape(BATCH, SEQ, D_MODEL)
```
